```python
import math
import jax, jax.numpy as jnp
from jax import lax
import numpy as np

D_MODEL = 2048
BATCH = 4
SEQ = 4096
DEPTH = 2

CTX_LEN = 256
GRID_W = 64
N_BRANCH = 4
BRANCH_W = 512
SHORT_K = 3
EPS = 1e-6

HY_ORDER = 2
HY_EMB = 33
HY_BANDS = (HY_EMB - 1) // 2
HY_FFN = 64
HY_FILTERS = HY_ORDER * 2 * BRANCH_W
HY_DECAY_MIN = -math.log(1e-2) / 1.5
HY_DECAY_MAX = -math.log(1e-2) / 0.3

ML_HEADS = 4
ML_HD = BRANCH_W // ML_HEADS
ML_CHUNK = 64

S5_GROUP = 16
S5_GROUPS = BRANCH_W // S5_GROUP
S5_STATE = 64

SSD_HD = 64
SSD_HEADS = BRANCH_W // SSD_HD
SSD_GROUPS = 2
SSD_STATE = 128
SSD_CHUNK = 128
SSD_XBC = BRANCH_W + 2 * SSD_GROUPS * SSD_STATE

N_EXPERTS = 16
EC_CAPACITY = 2
D_EXPERT = 1536

HY_OFF = 0
ML_OFF = HY_OFF + 3 * BRANCH_W
S5_OFF = ML_OFF + 4 * BRANCH_W + 4 * ML_HEADS
SSD_OFF = S5_OFF + BRANCH_W
GATE_OFF = SSD_OFF + BRANCH_W + SSD_XBC + 2 * SSD_HEADS
IN_WIDTH = GATE_OFF + N_BRANCH * D_MODEL

kernel_name = "hybrid_hyena_mlstm_s5_ssd_ecmoe_diffusion"


def f32(a):
    return a.astype(jnp.float32)


def rmsnorm(x, g):
    x32 = f32(x)
    y = x32 * lax.rsqrt(jnp.mean(x32 * x32, axis=-1, keepdims=True) + EPS)
    return (y * f32(g)).astype(x.dtype)


def modulate(h, shift, scale):
    return h * (1 + scale) + shift


def short_conv(u, w, b, rows):
    bsz, L, ch = u.shape
    if rows is not None:
        u = u.reshape(bsz * rows, GRID_W, ch)
    y = lax.conv_general_dilated(u, w[:, None, :].astype(u.dtype), window_strides=(1,), padding="SAME",
                                 dimension_numbers=("NWC", "WIO", "NWC"), feature_group_count=ch)
    return y.reshape(bsz, L, ch) + b.astype(u.dtype)


def hyena_filter_spectra(L, w1, b1, fr1, w2, b2, fr2, w3, decay):
    t = jnp.arange(L, dtype=jnp.float32)
    t_norm = t / (L - 1)
    freqs = jnp.linspace(1e-4, HY_BANDS - 1, HY_BANDS, dtype=jnp.float32)
    ang = (2.0 * math.pi / L) * t[:, None] * freqs[None, :]
    feats = jnp.concatenate([t_norm[:, None], jnp.cos(ang), -jnp.sin(ang)], axis=-1)
    hdn = jnp.sin(f32(fr1) * (feats @ f32(w1) + f32(b1)))
    hdn = jnp.sin(f32(fr2) * (hdn @ f32(w2) + f32(b2)))
    h = (hdn @ f32(w3)) * jnp.exp(-t_norm[:, None] * jnp.abs(f32(decay)))
    h = h.reshape(L, HY_ORDER, 2, BRANCH_W)
    h = h / jnp.sum(jnp.abs(h), axis=(0, 2), keepdims=True)
    filt = jnp.concatenate([h[:, :, 0], jnp.zeros((1, HY_ORDER, BRANCH_W), jnp.float32), h[:0:-1, :, 1]], axis=0)
    return jnp.fft.rfft(filt, axis=0)


def long_conv(z, spec, bias):
    L = z.shape[1]
    Z = jnp.fft.rfft(z, n=2 * L, axis=1)
    return jnp.fft.irfft(Z * spec[None], n=2 * L, axis=1)[:, :L] + bias * z


def hyena_branch(cols, rows, conv_w, conv_b, w1, b1, fr1, w2, b2, fr2, w3, decay, bias):
    L = cols.shape[1]
    u = f32(short_conv(cols, conv_w, conv_b, rows))
    v, x1, x2 = jnp.split(u, 3, axis=-1)
    spec = hyena_filter_spectra(L, w1, b1, fr1, w2, b2, fr2, w3, decay)
    bias = f32(bias)
    z = x1 * long_conv(v, spec[:, 0], bias[0])
    y = x2 * long_conv(z, spec[:, 1], bias[1])
    return y.astype(cols.dtype)


def mlstm_scan(q, k, v, ig, lf, state):
    bsz, L, H, d = q.shape
    nc = L // ML_CHUNK

    def chunks(a):
        a = a.reshape((bsz, nc, ML_CHUNK) + a.shape[2:])
        return jnp.moveaxis(jnp.moveaxis(a, 1, 0), 3, 2)

    tril = jnp.tril(jnp.ones((ML_CHUNK, ML_CHUNK), bool))

    def step(carry, xs):
        C, n, m = carry
        qc, kc, vc, ic, fc = xs
        b = jnp.cumsum(fc, axis=-1)
        a = b + m[..., None]
        dmat = jnp.where(tril, b[..., :, None] - b[..., None, :] + ic[..., None, :], -jnp.inf)
        mt = jnp.maximum(a, jnp.max(dmat, axis=-1))
        inter = jnp.exp(a - mt)
        s = jnp.einsum('bhtd,bhsd->bhts', qc, kc) * jnp.exp(dmat - mt[..., None])
        num = jnp.einsum('bhts,bhse->bhte', s, vc) + inter[..., None] * jnp.einsum('bhed,bhtd->bhte', C, qc)
        den = jnp.sum(s, axis=-1) + inter * jnp.einsum('bhd,bhtd->bht', n, qc)
        h = num / jnp.maximum(jnp.abs(den), jnp.exp(-mt))[..., None]
        m_new = mt[..., -1]
        ws = jnp.exp(b[..., -1:] - b + ic - m_new[..., None])
        dec = jnp.exp(b[..., -1] + m - m_new)
        C_new = dec[..., None, None] * C + jnp.einsum('bhs,bhse,bhsd->bhed', ws, vc, kc)
        n_new = dec[..., None] * n + jnp.einsum('bhs,bhsd->bhd', ws, kc)
        return (C_new, n_new, m_new), h

    state, hs = lax.scan(step, state, (chunks(q), chunks(k), chunks(v), chunks(ig), chunks(lf)))
    hs = jnp.moveaxis(jnp.moveaxis(hs, 2, 3), 0, 1).reshape(bsz, L, H, d)
    return hs, state


def mlstm_branch(cols_ctx, cols_lat, rows, with_ctx_out, conv_w, conv_b, gate_b, norm_g):
    W = BRANCH_W

    def prep(cols, rws):
        bsz, L, _ = cols.shape
        qk = f32(jax.nn.silu(short_conv(cols[..., :2 * W], conv_w, conv_b, rws)))
        heads = lambda a: f32(a).reshape(bsz, L, ML_HEADS, ML_HD)
        q = heads(qk[..., :W])
        k = heads(qk[..., W:]) * (ML_HD ** -0.5)
        v = heads(cols[..., 2 * W:3 * W])
        o = cols[..., 3 * W:4 * W]
        g = f32(cols[..., 4 * W:]).reshape(bsz, L, 4, ML_HEADS) + f32(gate_b)
        return q, k, v, o, g

    def bidir(q, k, v, g, st_f, st_b):
        flip = lambda a: a[:, ::-1]
        h_f, st_f = mlstm_scan(q, k, v, g[:, :, 0], jax.nn.log_sigmoid(g[:, :, 1]), st_f)
        h_b, st_b = mlstm_scan(flip(q), flip(k), flip(v), flip(g[:, :, 2]),
                               jax.nn.log_sigmoid(flip(g[:, :, 3])), st_b)
        return h_f + flip(h_b), st_f, st_b

    def finish(h, o):
        bsz, L = h.shape[:2]
        h = h * lax.rsqrt(jnp.mean(h * h, axis=-1, keepdims=True) + EPS)
        h = h.reshape(bsz, L, W) * f32(norm_g) * jax.nn.sigmoid(f32(o))
        return h.astype(o.dtype)

    bsz = cols_ctx.shape[0]
    zero = (jnp.zeros((bsz, ML_HEADS, ML_HD, ML_HD), jnp.float32),
            jnp.zeros((bsz, ML_HEADS, ML_HD), jnp.float32),
            jnp.zeros((bsz, ML_HEADS), jnp.float32))
    qc, kc, vc, oc, gc = prep(cols_ctx, None)
    h_ctx, st_f, st_b = bidir(qc, kc, vc, gc, zero, zero)
    ql, kl, vl, ol, gl = prep(cols_lat, rows)
    h_lat, _, _ = bidir(ql, kl, vl, gl, st_f, st_b)
    y_ctx = finish(h_ctx, oc) if with_ctx_out else None
    return y_ctx, finish(h_lat, ol)


def s5_discretise(a_re, a_im, log_dt, b_re, b_im):
    lam_re = jnp.minimum(a_re, -1e-4)
    dt = jnp.exp(log_dt)[:, None]
    mag = jnp.exp(lam_re * dt)
    ab_re, ab_im = mag * jnp.cos(a_im * dt), mag * jnp.sin(a_im * dt)
    den = lam_re * lam_re + a_im * a_im
    nr, ni = ab_re - 1.0, ab_im
    f_re = (nr * lam_re + ni * a_im) / den
    f_im = (ni * lam_re - nr * a_im) / den
    bb_re = f_re[..., None] * b_re - f_im[..., None] * b_im
    bb_im = f_re[..., None] * b_im + f_im[..., None] * b_re
    return ab_re, ab_im, bb_re, bb_im


def s5_scan(u, ab_re, ab_im, bb_re, bb_im, s0, reverse):
    L = u.shape[1]
    bu_re = jnp.einsum('blgc,gnc->blgn', u, bb_re)
    bu_im = jnp.einsum('blgc,gnc->blgn', u, bb_im)
    a_re = jnp.broadcast_to(ab_re, (1, L) + ab_re.shape)
    a_im = jnp.broadcast_to(ab_im, (1, L) + ab_im.shape)

    def combine(e1, e2):
        a1r, a1i, b1r, b1i = e1
        a2r, a2i, b2r, b2i = e2
        return (a2r * a1r - a2i * a1i, a2r * a1i + a2i * a1r,
                a2r * b1r - a2i * b1i + b2r, a2r * b1i + a2i * b1r + b2i)

    ac_re, ac_im, x_re, x_im = lax.associative_scan(combine, (a_re, a_im, bu_re, bu_im), reverse=reverse, axis=1)
    if s0 is not None:
        s_re, s_im = s0[0][:, None], s0[1][:, None]
        x_re = x_re + ac_re * s_re - ac_im * s_im
        x_im = x_im + ac_re * s_im + ac_im * s_re
    return x_re, x_im


def s5_branch(u_ctx, u_lat, with_ctx_out, a_re, a_im, log_dt, b_re, b_im, c_re, c_im, d_skip, glu_w, glu_b):
    disc = [s5_discretise(f32(a_re[d]), f32(a_im[d]), f32(log_dt[d]), f32(b_re), f32(b_im)) for d in range(2)]
    c_re, c_im = f32(c_re), f32(c_im)

    def run_states(u, inits):
        bsz, L, _ = u.shape
        ug = f32(u).reshape(bsz, L, S5_GROUPS, S5_GROUP)
        fr, fi = s5_scan(ug, *disc[0], inits[0], reverse=False)
        br, bi = s5_scan(ug, *disc[1], inits[1], reverse=True)
        finals = ((fr[:, -1], fi[:, -1]), (br[:, 0], bi[:, 0]))
        return fr + br, fi + bi, finals

    def readout(u, s_re, s_im):
        bsz, L, _ = u.shape
        y = jnp.einsum('blgn,gcn->blgc', s_re, c_re) - jnp.einsum('blgn,gcn->blgc', s_im, c_im)
        y = y.reshape(bsz, L, BRANCH_W) + f32(d_skip) * f32(u)
        g = jax.nn.gelu(y)
        return (g * jax.nn.sigmoid(g @ f32(glu_w) + f32(glu_b))).astype(u.dtype)

    sc_re, sc_im, finals = run_states(u_ctx, (None, None))
    sl_re, sl_im, _ = run_states(u_lat, finals)
    y_ctx = readout(u_ctx, sc_re, sc_im) if with_ctx_out else None
    return y_ctx, readout(u_lat, sl_re, sl_im)


def segsum(a):
    T = a.shape[-1]
    cs = jnp.cumsum(a, axis=-1)
    mask = jnp.tril(jnp.ones((T, T), bool))
    return jnp.where(mask, cs[..., :, None] - cs[..., None, :], -jnp.inf)


def ssd_scan(x, dt, A, Bm, Cm, init):
    bsz, L, H, P = x.shape
    nc, l = L // SSD_CHUNK, SSD_CHUNK
    X = (x * dt[..., None]).reshape(bsz, nc, l, H, P)
    Bc = Bm.reshape(bsz, nc, l, H, -1)
    Cc = Cm.reshape(bsz, nc, l, H, -1)
    a = jnp.moveaxis((dt * A).reshape(bsz, nc, l, H), 3, 1)
    a_cs = jnp.cumsum(a, axis=-1)
    scores = jnp.einsum('bclhn,bcshn->bhcls', Cc, Bc) * jnp.exp(segsum(a))
    y_diag = jnp.einsum('bhcls,bcshp->bclhp', scores, X)
    decay_states = jnp.moveaxis(jnp.exp(a_cs[..., -1:] - a_cs), 1, 3)
    states = jnp.einsum('bclhn,bclhp->bchpn', Bc * decay_states[..., None], X)
    states = jnp.concatenate([init[:, None], states], axis=1)
    chunk_decay = jnp.exp(segsum(jnp.pad(a_cs[..., -1], ((0, 0), (0, 0), (1, 0)))))
    states = jnp.einsum('bhzc,bchpn->bzhpn', chunk_decay, states)
    y_off = jnp.einsum('bclhn,bchpn->bclhp', Cc, states[:, :-1]) * jnp.moveaxis(jnp.exp(a_cs), 1, 3)[..., None]
    return (y_diag + y_off).reshape(bsz, L, H, P), states[:, -1]


def ssd_branch(cols_ctx, cols_lat, rows, with_ctx_out, conv_w, conv_b, dt_bias, a_log, d_skip, norm_g):
    W = BRANCH_W
    GN = SSD_GROUPS * SSD_STATE
    rep = SSD_HEADS // SSD_GROUPS
    A = -jnp.exp(f32(a_log))

    def prep(cols, rws):
        bsz, L, _ = cols.shape
        xbc = f32(jax.nn.silu(short_conv(cols[..., W:W + SSD_XBC], conv_w, conv_b, rws)))
        x = xbc[..., :W].reshape(bsz, L, SSD_HEADS, SSD_HD)
        Bm = jnp.repeat(xbc[..., W:W + GN].reshape(bsz, L, SSD_GROUPS, SSD_STATE), rep, axis=2)
        Cm = jnp.repeat(xbc[..., W + GN:].reshape(bsz, L, SSD_GROUPS, SSD_STATE), rep, axis=2)
        dt = jax.nn.softplus(f32(cols[..., W + SSD_XBC:]).reshape(bsz, L, 2, SSD_HEADS) + f32(dt_bias))
        return x, Bm, Cm, dt

    def bidir(x, Bm, Cm, dt, s_f, s_b):
        flip = lambda a: a[:, ::-1]
        y_f, s_f = ssd_scan(x, dt[:, :, 0], A[0], Bm, Cm, s_f)
        y_b, s_b = ssd_scan(flip(x), flip(dt[:, :, 1]), A[1], flip(Bm), flip(Cm), s_b)
        return y_f + flip(y_b) + f32(d_skip)[:, None] * x, s_f, s_b

    def finish(y, cols):
        bsz, L = y.shape[:2]
        z = f32(cols[..., :W])
        return rmsnorm(y.reshape(bsz, L, W) * jax.nn.silu(z), norm_g).astype(cols.dtype)

    bsz = cols_ctx.shape[0]
    zero = jnp.zeros((bsz, SSD_HEADS, SSD_HD, SSD_STATE), jnp.float32)
    xc, Bc, Cc, dtc = prep(cols_ctx, None)
    y_ctx, s_f, s_b = bidir(xc, Bc, Cc, dtc, zero, zero)
    xl, Bl, Cl, dtl = prep(cols_lat, rows)
    y_lat, _, _ = bidir(xl, Bl, Cl, dtl, s_f, s_b)
    out_ctx = finish(y_ctx, cols_ctx) if with_ctx_out else None
    return out_ctx, finish(y_lat, cols_lat)


def token_mixers(h_ctx, h_lat, rows, with_ctx_out, w_in, hy_p, ml_p, s5_p, ssd_p, w_branch, w_out):
    p_ctx = h_ctx @ w_in
    p_lat = h_lat @ w_in
    hy_lat = hyena_branch(p_lat[..., HY_OFF:ML_OFF], rows, *hy_p)
    hy_ctx = hyena_branch(p_ctx[..., HY_OFF:ML_OFF], None, *hy_p) if with_ctx_out else None
    ml_ctx, ml_lat = mlstm_branch(p_ctx[..., ML_OFF:S5_OFF], p_lat[..., ML_OFF:S5_OFF], rows, with_ctx_out, *ml_p)
    s5_ctx, s5_lat = s5_branch(p_ctx[..., S5_OFF:SSD_OFF], p_lat[..., S5_OFF:SSD_OFF], with_ctx_out, *s5_p)
    ssd_ctx, ssd_lat = ssd_branch(p_ctx[..., SSD_OFF:GATE_OFF], p_lat[..., SSD_OFF:GATE_OFF], rows, with_ctx_out, *ssd_p)

    def merge(p, branches):
        acc = None
        for k, y in enumerate(branches):
            g = jax.nn.sigmoid(p[..., GATE_OFF + k * D_MODEL:GATE_OFF + (k + 1) * D_MODEL])
            term = g * (y @ w_branch[k])
            acc = term if acc is None else acc + term
        return acc @ w_out

    y_lat = merge(p_lat, (hy_lat, ml_lat, s5_lat, ssd_lat))
    y_ctx = merge(p_ctx, (hy_ctx, ml_ctx, s5_ctx, ssd_ctx)) if with_ctx_out else None
    return y_ctx, y_lat


def ec_moe(h, w_router, w_gate, w_up, w_down):
    bsz, n, _ = h.shape
    cap = EC_CAPACITY * n // N_EXPERTS
    probs = jax.nn.softmax(f32(h @ w_router), axis=-1)
    gate, idx = lax.top_k(jnp.swapaxes(probs, 1, 2), cap)
    bidx = jnp.arange(bsz)[:, None, None]
    xs = h[bidx, idx]
    hid = jax.nn.silu(jnp.einsum('becd,edf->becf', xs, w_gate)) * jnp.einsum('becd,edf->becf', xs, w_up)
    y = jnp.einsum('becf,efd->becd', hid, w_down) * gate[..., None].astype(h.dtype)
    return jnp.zeros_like(h).at[bidx, idx].add(y)


def setup_inputs(seed: int = 0) -> dict:
    key = jax.random.key(seed)
    ks = iter(jax.random.split(key, 64))

    def nrm(shape, scale):
        return scale * jax.random.normal(next(ks), shape, jnp.float32)

    def unif(shape, lo, hi):
        return jax.random.uniform(next(ks), shape, jnp.float32, lo, hi)

    D, W, Lr = D_MODEL, BRANCH_W, DEPTH
    inp = {}
    inp["x"] = nrm((BATCH, SEQ, D), 1.0)
    inp["c"] = nrm((BATCH, D), 1.0)
    inp["ctx"] = nrm((BATCH, CTX_LEN, D), 1.0)
    inp["c_ctx"] = nrm((D,), 1.0)
    inp["ada_w"] = nrm((Lr, D, 6 * D), 0.5 * D ** -0.5)
    inp["ada_b"] = nrm((Lr, 6 * D), 0.01)
    inp["norm_mix"] = 1.0 + nrm((Lr, D), 0.05)
    inp["norm_ffn"] = 1.0 + nrm((Lr, D), 0.05)
    inp["w_in"] = nrm((Lr, D, IN_WIDTH), D ** -0.5)
    inp["hy_conv_w"] = nrm((Lr, SHORT_K, 3 * W), SHORT_K ** -0.5)
    inp["hy_conv_b"] = nrm((Lr, 3 * W), 0.02)
    inp["hy_w1"] = nrm((Lr, HY_EMB, HY_FFN), HY_EMB ** -0.5)
    inp["hy_b1"] = nrm((Lr, HY_FFN), 0.1)
    inp["hy_fr1"] = 1.0 + nrm((Lr, HY_FFN), 0.01)
    inp["hy_w2"] = nrm((Lr, HY_FFN, HY_FFN), HY_FFN ** -0.5)
    inp["hy_b2"] = nrm((Lr, HY_FFN), 0.1)
    inp["hy_fr2"] = 1.0 + nrm((Lr, HY_FFN), 0.01)
    inp["hy_w3"] = nrm((Lr, HY_FFN, HY_FILTERS), HY_FFN ** -0.5)
    inp["hy_decay"] = jnp.tile(jnp.linspace(HY_DECAY_MIN, HY_DECAY_MAX, W, dtype=jnp.float32), HY_ORDER * 2)[None] + nrm((Lr, HY_FILTERS), 0.1)
    inp["hy_bias"] = nrm((Lr, HY_ORDER, W), 0.5)
    inp["ml_conv_w"] = nrm((Lr, SHORT_K, 2 * W), SHORT_K ** -0.5)
    inp["ml_conv_b"] = nrm((Lr, 2 * W), 0.02)
    fb = jnp.linspace(3.0, 6.0, ML_HEADS, dtype=jnp.float32)
    zb = jnp.zeros((ML_HEADS,), jnp.float32)
    inp["ml_gate_b"] = jnp.stack([zb, fb, zb, fb])[None] + nrm((Lr, 4, ML_HEADS), 0.1)
    inp["ml_norm"] = 1.0 + nrm((Lr, W), 0.05)
    inp["s5_a_re"] = -0.5 + nrm((Lr, 2, S5_GROUPS, S5_STATE), 0.01)
    inp["s5_a_im"] = math.pi * jnp.arange(S5_STATE, dtype=jnp.float32) + nrm((Lr, 2, S5_GROUPS, S5_STATE), 0.01)
    inp["s5_log_dt"] = unif((Lr, 2, S5_GROUPS), math.log(1e-3), math.log(1e-1))
    inp["s5_b_re"] = nrm((Lr, S5_GROUPS, S5_STATE, S5_GROUP), (2 * S5_GROUP) ** -0.5)
    inp["s5_b_im"] = nrm((Lr, S5_GROUPS, S5_STATE, S5_GROUP), (2 * S5_GROUP) ** -0.5)
    inp["s5_c_re"] = nrm((Lr, S5_GROUPS, S5_GROUP, S5_STATE), (2 * S5_STATE) ** -0.5)
    inp["s5_c_im"] = nrm((Lr, S5_GROUPS, S5_GROUP, S5_STATE), (2 * S5_STATE) ** -0.5)
    inp["s5_d"] = nrm((Lr, W), 0.5)
    inp["s5_glu_w"] = nrm((Lr, W, W), W ** -0.5)
    inp["s5_glu_b"] = nrm((Lr, W), 0.01)
    inp["ssd_conv_w"] = nrm((Lr, SHORT_K, SSD_XBC), SHORT_K ** -0.5)
    inp["ssd_conv_b"] = nrm((Lr, SSD_XBC), 0.02)
    dt0 = jnp.exp(unif((Lr, 2, SSD_HEADS), math.log(1e-3), math.log(1e-1)))
    inp["ssd_dt_bias"] = dt0 + jnp.log(-jnp.expm1(-dt0))
    inp["ssd_a_log"] = jnp.log(unif((Lr, 2, SSD_HEADS), 1.0, 16.0))
    inp["ssd_d"] = 1.0 + nrm((Lr, SSD_HEADS), 0.1)
    inp["ssd_norm"] = 1.0 + nrm((Lr, W), 0.05)
    inp["w_branch"] = nrm((Lr, N_BRANCH, W, D), W ** -0.5)
    inp["w_out"] = nrm((Lr, D, D), D ** -0.5)
    inp["w_router"] = nrm((Lr, D, N_EXPERTS), D ** -0.5)
    inp["w_gate"] = nrm((Lr, N_EXPERTS, D, D_EXPERT), D ** -0.5)
    inp["w_up"] = nrm((Lr, N_EXPERTS, D, D_EXPERT), D ** -0.5)
    inp["w_down"] = nrm((Lr, N_EXPERTS, D_EXPERT, D), D_EXPERT ** -0.5)
    inp["final_norm"] = 1.0 + nrm((D,), 0.05)
    return inp


def reference(x, c, ctx, c_ctx, ada_w, ada_b, norm_mix, norm_ffn, w_in,
              hy_conv_w, hy_conv_b, hy_w1, hy_b1, hy_fr1, hy_w2, hy_b2, hy_fr2, hy_w3, hy_decay, hy_bias,
              ml_conv_w, ml_conv_b, ml_gate_b, ml_norm,
              s5_a_re, s5_a_im, s5_log_dt, s5_b_re, s5_b_im, s5_c_re, s5_c_im, s5_d, s5_glu_w, s5_glu_b,
              ssd_conv_w, ssd_conv_b, ssd_dt_bias, ssd_a_log, ssd_d, ssd_norm,
              w_branch, w_out, w_router, w_gate, w_up, w_down, final_norm):
    rows = x.shape[1] // GRID_W
    for i in range(DEPTH):
        last = i == DEPTH - 1
        m_lat = (jax.nn.silu(c) @ ada_w[i] + ada_b[i]).reshape(c.shape[0], 6, 1, D_MODEL)
        m_ctx = (jax.nn.silu(c_ctx) @ ada_w[i] + ada_b[i]).reshape(6, 1, 1, D_MODEL)
        hy_p = (hy_conv_w[i], hy_conv_b[i], hy_w1[i], hy_b1[i], hy_fr1[i], hy_w2[i], hy_b2[i], hy_fr2[i],
                hy_w3[i], hy_decay[i], hy_bias[i])
        ml_p = (ml_conv_w[i], ml_conv_b[i], ml_gate_b[i], ml_norm[i])
        s5_p = (s5_a_re[i], s5_a_im[i], s5_log_dt[i], s5_b_re[i], s5_b_im[i], s5_c_re[i], s5_c_im[i],
                s5_d[i], s5_glu_w[i], s5_glu_b[i])
        ssd_p = (ssd_conv_w[i], ssd_conv_b[i], ssd_dt_bias[i], ssd_a_log[i], ssd_d[i], ssd_norm[i])
        moe_p = (w_router[i], w_gate[i], w_up[i], w_down[i])

        h_lat = modulate(rmsnorm(x, norm_mix[i]), m_lat[:, 0], m_lat[:, 1])
        h_ctx = modulate(rmsnorm(ctx, norm_mix[i]), m_ctx[0], m_ctx[1])
        y_ctx, y_lat = token_mixers(h_ctx, h_lat, rows, not last, w_in[i], hy_p, ml_p, s5_p, ssd_p,
                                    w_branch[i], w_out[i])
        x = x + m_lat[:, 2] * y_lat
        x = x + m_lat[:, 5] * ec_moe(modulate(rmsnorm(x, norm_ffn[i]), m_lat[:, 3], m_lat[:, 4]), *moe_p)
        if not last:
            ctx = ctx + m_ctx[2] * y_ctx
            ctx = ctx + m_ctx[5] * ec_moe(modulate(rmsnorm(ctx, norm_ffn[i]), m_ctx[3], m_ctx[4]), *moe_p)
    return rmsnorm(x, final_norm)
```

```python
import functools
import math

import jax
import jax.numpy as jnp
from jax import lax
from jax.experimental import pallas as pl
from jax.experimental.pallas import tpu as pltpu

F32 = jnp.float32
BF16 = jnp.bfloat16
I32 = jnp.int32

D_MODEL = 2048
W = 512
GRID_W = 64
EPS = 1e-6
HY_EMB = 33
HY_BANDS = 16
HY_FFN = 64
ML_HEADS = 4
ML_HD = 128
ML_CHUNK = 64
S5_GROUP = 16
S5_GROUPS = 32
S5_STATE = 64
S5_NS = S5_GROUPS * S5_STATE
SSD_HD = 64
SSD_HEADS = 8
SSD_GROUPS = 2
SSD_STATE = 128
SSD_CHUNK = 128
N_EXPERTS = 16
EC_CAPACITY = 2
D_EXPERT = 1536

P_HY = 0
P_MLQK = 1536
P_MLV = 2560
P_MLO = 3072
P_S5 = 3584
P_SSDZ = 4096
P_XBC = 4608
P_GATE = 5632
P_SMALL = 13824
P_WIDTH = 14336

LANES = 128
VMEM_LIMIT_BYTES = 56 * 1024 * 1024


def _cparams(*sem):
    return pltpu.CompilerParams(dimension_semantics=sem, vmem_limit_bytes=VMEM_LIMIT_BYTES)


def _dot(a, b):
    return jnp.dot(a, b, preferred_element_type=F32)


def _dot_nt(a, b):
    return lax.dot_general(a, b, (((1,), (1,)), ((), ())), preferred_element_type=F32)


def _split3(x):
    hi = x.astype(BF16)
    r = x - hi.astype(F32)
    mid = r.astype(BF16)
    lo = (r - mid.astype(F32)).astype(BF16)
    return hi, mid, lo


def _dot_sel(m01, x):
    hi, mid, lo = _split3(x)
    return _dot(m01, hi) + _dot(m01, mid) + _dot(m01, lo)


def _sel_dot(x, m01):
    hi, mid, lo = _split3(x)
    return _dot(hi, m01) + _dot(mid, m01) + _dot(lo, m01)


def _dot3(a, b):
    ah = a.astype(BF16)
    al = (a - ah.astype(F32)).astype(BF16)
    bh = b.astype(BF16)
    bl = (b - bh.astype(F32)).astype(BF16)
    return _dot(ah, bh) + _dot(al, bh) + _dot(ah, bl)


def _sigmoid(x):
    return 1.0 / (1.0 + jnp.exp(-x))


def _silu(x):
    return x * _sigmoid(x)


def _softplus(x):
    return jnp.maximum(x, 0.0) + jnp.log(1.0 + jnp.exp(-jnp.abs(x)))


def _log_sigmoid(x):
    return jnp.minimum(x, 0.0) - jnp.log(1.0 + jnp.exp(-jnp.abs(x)))


def _gelu_tanh(x):
    return 0.5 * x * (1.0 + jnp.tanh(math.sqrt(2.0 / math.pi) * (x + 0.044715 * (x * x * x))))


def _tile(n, pref):
    t = min(n, pref)
    while n % t:
        t //= 2
    return t


def _mm_body(a_ref, b_ref, o_ref):
    o_ref[...] = _dot(a_ref[...].astype(BF16), b_ref[...].astype(BF16)).astype(o_ref.dtype)


def _matmul(a, b, tm, tn, out_dtype=F32):
    m, k = a.shape
    n = b.shape[1]
    tm, tn = _tile(m, tm), _tile(n, tn)
    return pl.pallas_call(
        _mm_body,
        grid=(m // tm, n // tn),
        in_specs=[pl.BlockSpec((tm, k), lambda i, j: (i, 0)), pl.BlockSpec((k, tn), lambda i, j: (0, j))],
        out_specs=pl.BlockSpec((tm, tn), lambda i, j: (i, j)),
        out_shape=jax.ShapeDtypeStruct((m, n), out_dtype),
        compiler_params=_cparams("parallel", "parallel"),
    )(a, b)


def _ada_body(c_ref, w_ref, b_ref, o_ref):
    c = c_ref[...]
    o_ref[...] = _dot3(_silu(c), w_ref[...]) + b_ref[...]


def _ada(cc, w, b):
    d, n = w.shape
    tn = 1024
    return pl.pallas_call(
        _ada_body,
        grid=(n // tn,),
        in_specs=[pl.BlockSpec((8, d), lambda j: (0, 0)), pl.BlockSpec((d, tn), lambda j: (0, j)),
                  pl.BlockSpec((1, tn), lambda j: (0, j))],
        out_specs=pl.BlockSpec((8, tn), lambda j: (0, j)),
        out_shape=jax.ShapeDtypeStruct((8, n), F32),
        compiler_params=_cparams("parallel"),
    )(cc, w, b)


def _norm_mod(x, g, shift, scale):
    y = x * lax.rsqrt(jnp.mean(x * x, axis=-1, keepdims=True) + EPS) * g
    return y * (1.0 + scale) + shift


def _inproj_body(xl_ref, xc_ref, g_ref, mod_ref, w_ref, o_ref, h_scr, *, n_lat_tiles):
    i = pl.program_id(0)

    @pl.when(pl.program_id(1) == 0)
    def _():
        @pl.when(i < n_lat_tiles)
        def _():
            h_scr[...] = _norm_mod(xl_ref[...], g_ref[...], mod_ref[0:1, :], mod_ref[1:2, :]).astype(BF16)

        @pl.when(i >= n_lat_tiles)
        def _():
            h_scr[...] = _norm_mod(xc_ref[...], g_ref[...], mod_ref[0:1, :], mod_ref[1:2, :]).astype(BF16)

    o_ref[...] = _dot(h_scr[...], w_ref[...])


def _mod_index(i, tm, n_lat_tiles, seq, bsz):
    return jnp.where(i < n_lat_tiles, (i * tm) // seq, bsz)


def _inproj(xl, xc, g, mods, w, seq, bsz):
    nl, nc = xl.shape[0], xc.shape[0]
    d, n = w.shape
    tm = _tile(math.gcd(nl, nc), 1024)
    tn = 512
    n_lat = nl // tm
    n_ctx = nc // tm
    return pl.pallas_call(
        functools.partial(_inproj_body, n_lat_tiles=n_lat),
        grid=(n_lat + n_ctx, n // tn),
        in_specs=[
            pl.BlockSpec((tm, d), lambda i, j: (jnp.minimum(i, n_lat - 1), 0)),
            pl.BlockSpec((tm, d), lambda i, j: (jnp.maximum(i - n_lat, 0), 0)),
            pl.BlockSpec((1, d), lambda i, j: (0, 0)),
            pl.BlockSpec((None, 6, d), lambda i, j: (_mod_index(i, tm, n_lat, seq, bsz), 0, 0)),
            pl.BlockSpec((d, tn), lambda i, j: (0, j)),
        ],
        out_specs=pl.BlockSpec((tm, tn), lambda i, j: (i, j)),
        out_shape=jax.ShapeDtypeStruct((nl + nc, n), F32),
        scratch_shapes=[pltpu.VMEM((tm, d), BF16)],
        compiler_params=_cparams("parallel", "arbitrary"),
    )(xl, xc, g, mods, w)


def _conv_body(u_ref, w_ref, b_ref, s_ref, o_ref, *, act, n_lat_tiles, tile):
    i = pl.program_id(0)
    u = u_ref[...]
    t = lax.broadcasted_iota(I32, u.shape, 0)
    pos = jnp.where(i < n_lat_tiles, t & (GRID_W - 1), t)
    last = jnp.where(i < n_lat_tiles, GRID_W - 1, tile - 1)
    prev = jnp.where(pos == 0, 0.0, pltpu.roll(u, 1, 0))
    nxt = jnp.where(pos == last, 0.0, pltpu.roll(u, tile - 1, 0))
    y = w_ref[0:1, :] * prev + w_ref[1:2, :] * u + w_ref[2:3, :] * nxt + b_ref[...]
    if act:
        y = _silu(y)
    o_ref[...] = y * s_ref[...]


def _short_conv(p, col0, width, w, b, post, act, n_lat_rows, ctx_len):
    rows = p.shape[0]
    tile = ctx_len
    cb = 512
    return pl.pallas_call(
        functools.partial(_conv_body, act=act, n_lat_tiles=n_lat_rows // tile, tile=tile),
        grid=(rows // tile, width // cb),
        in_specs=[
            pl.BlockSpec((tile, cb), lambda i, j: (i, col0 // cb + j)),
            pl.BlockSpec((3, cb), lambda i, j: (0, j)),
            pl.BlockSpec((1, cb), lambda i, j: (0, j)),
            pl.BlockSpec((1, cb), lambda i, j: (0, j)),
        ],
        out_specs=pl.BlockSpec((tile, cb), lambda i, j: (i, j)),
        out_shape=jax.ShapeDtypeStruct((rows, width), F32),
        compiler_params=_cparams("parallel", "parallel"),
    )(p, w, b, post)


def _hy_filter_body(feat_ref, w1_ref, b1_ref, fr1_ref, w2_ref, b2_ref, fr2_ref, w3a_ref, w3b_ref,
                    da_ref, db_ref, gp_ref, gm_ref, *, seq):
    hdn = jnp.sin(fr1_ref[...] * (_dot3(feat_ref[...], w1_ref[...]) + b1_ref[...]))
    hdn = jnp.sin(fr2_ref[...] * (_dot3(hdn, w2_ref[...]) + b2_ref[...]))
    t = lax.broadcasted_iota(I32, (seq, 1), 0)
    tn = t.astype(F32) / float(seq - 1)
    hf = _dot3(hdn, w3a_ref[...]) * jnp.exp(-tn * jnp.abs(da_ref[...]))
    hb = _dot3(hdn, w3b_ref[...]) * jnp.exp(-tn * jnp.abs(db_ref[...]))
    norm = jnp.sum(jnp.abs(hf) + jnp.abs(hb), axis=0, keepdims=True)
    hf = hf / norm
    hb = jnp.where(t == 0, 0.0, hb / norm)
    gp_ref[...] = hf + hb
    gm_ref[...] = hf - hb


def _hy_filters(seq, w1, b1, fr1, w2, b2, fr2, w3, decay):
    t = jnp.arange(seq, dtype=F32)
    freqs = jnp.linspace(1e-4, HY_BANDS - 1, HY_BANDS, dtype=F32)
    ang = (2.0 * math.pi / seq) * t[:, None] * freqs[None, :]
    feats = jnp.concatenate([(t / (seq - 1))[:, None], jnp.cos(ang), -jnp.sin(ang)], axis=-1)
    hp = LANES
    feats = jnp.pad(feats, ((0, 0), (0, hp - HY_EMB)))
    w1p = jnp.pad(w1, ((0, hp - HY_EMB), (0, hp - HY_FFN)))
    w2p = jnp.pad(w2, ((0, hp - HY_FFN), (0, hp - HY_FFN)))
    w3p = jnp.pad(w3, ((0, hp - HY_FFN), (0, 0))).reshape(hp, 4, W).transpose(1, 0, 2)
    row = lambda v: jnp.pad(v, (0, hp - HY_FFN)).reshape(1, hp)
    dec = decay.reshape(4, 1, W)
    cb = 256
    full = lambda shape: pl.BlockSpec(shape, lambda o, j: (0,) * len(shape))
    gp, gm = pl.pallas_call(
        functools.partial(_hy_filter_body, seq=seq),
        grid=(2, W // cb),
        in_specs=[
            full((seq, hp)), full((hp, hp)), full((1, hp)), full((1, hp)),
            full((hp, hp)), full((1, hp)), full((1, hp)),
            pl.BlockSpec((None, hp, cb), lambda o, j: (2 * o, 0, j)),
            pl.BlockSpec((None, hp, cb), lambda o, j: (2 * o + 1, 0, j)),
            pl.BlockSpec((None, 1, cb), lambda o, j: (2 * o, 0, j)),
            pl.BlockSpec((None, 1, cb), lambda o, j: (2 * o + 1, 0, j)),
        ],
        out_specs=[pl.BlockSpec((seq, cb), lambda o, j: (0, o * (W // cb) + j))] * 2,
        out_shape=[jax.ShapeDtypeStruct((seq, 2 * W), F32)] * 2,
        compiler_params=_cparams("parallel", "parallel"),
    )(feats, w1p, row(b1), row(fr1), w2p, row(b2), row(fr2), w3p, w3p, dec, dec)
    return gp, gm


def _dft_tables(seq):
    k = jnp.arange(seq, dtype=I32)
    m = ((2 * k[:, None] + 1) * k[None, :]) % (4 * seq)
    ang = m.astype(F32) * (math.pi / (2 * seq))
    c, s = jnp.cos(ang), jnp.sin(ang)
    return c.astype(BF16), s.astype(BF16), c.T.astype(BF16), s.T.astype(BF16)


def _hy_fwd_body(c_ref, s_ref, z_ref, gc_ref, gs_ref, p1_ref, p2_ref):
    z = z_ref[...].astype(BF16)
    zc = _dot(c_ref[...], z)
    zs = _dot(s_ref[...], z)
    gc, gs = gc_ref[...], gs_ref[...]
    p1_ref[...] = (zc * gc - zs * gs).astype(BF16)
    p2_ref[...] = (zc * gs + zs * gc).astype(BF16)


def _hy_fwd(ctab, stab, z, zcol, zrow0, gc, gs, gcol, seq, nseq):
    tk = _tile(seq, 512)
    nk = seq // tk
    zb0 = zrow0 // seq
    return pl.pallas_call(
        _hy_fwd_body,
        grid=(nseq, nk),
        in_specs=[
            pl.BlockSpec((tk, seq), lambda b, k: (k, 0)),
            pl.BlockSpec((tk, seq), lambda b, k: (k, 0)),
            pl.BlockSpec((seq, W), lambda b, k: (zb0 + b, zcol)),
            pl.BlockSpec((tk, W), lambda b, k: (k, gcol)),
            pl.BlockSpec((tk, W), lambda b, k: (k, gcol)),
        ],
        out_specs=[pl.BlockSpec((tk, W), lambda b, k: (b * nk + k, 0))] * 2,
        out_shape=[jax.ShapeDtypeStruct((nseq * seq, W), BF16)] * 2,
        compiler_params=_cparams("parallel", "arbitrary"),
    )(ctab, stab, z, gc, gs)


def _hy_inv_body(ct_ref, st_ref, p1_ref, p2_ref, zin_ref, mul_ref, bias_ref, o_ref, *, seq):
    y = (_dot(ct_ref[...], p1_ref[...]) + _dot(st_ref[...], p2_ref[...])) * (1.0 / seq)
    o_ref[...] = mul_ref[...] * (y + bias_ref[...] * zin_ref[...])


def _hy_inv(cttab, sttab, p1, p2, zin, zin_col, zin_row0, mul, mul_col, mul_row0, bias, seq, nseq):
    tt = _tile(seq, 512)
    nt = seq // tt
    zr, mr = zin_row0 // tt, mul_row0 // tt
    return pl.pallas_call(
        functools.partial(_hy_inv_body, seq=seq),
        grid=(nseq, nt),
        in_specs=[
            pl.BlockSpec((tt, seq), lambda b, t: (t, 0)),
            pl.BlockSpec((tt, seq), lambda b, t: (t, 0)),
            pl.BlockSpec((seq, W), lambda b, t: (b, 0)),
            pl.BlockSpec((seq, W), lambda b, t: (b, 0)),
            pl.BlockSpec((tt, W), lambda b, t: (zr + b * nt + t, zin_col)),
            pl.BlockSpec((tt, W), lambda b, t: (mr + b * nt + t, mul_col)),
            pl.BlockSpec((1, W), lambda b, t: (0, 0)),
        ],
        out_specs=pl.BlockSpec((tt, W), lambda b, t: (b * nt + t, 0)),
        out_shape=jax.ShapeDtypeStruct((nseq * seq, W), F32),
        compiler_params=_cparams("parallel", "arbitrary"),
    )(cttab, sttab, p1, p2, zin, mul, bias)


def _hyena(u, row0, seq, nseq, tabs, filt):
    ctab, stab, cttab, sttab = tabs
    gc, gs, bias = filt
    p1, p2 = _hy_fwd(ctab, stab, u, 0, row0, gc, gs, 0, seq, nseq)
    z2 = _hy_inv(cttab, sttab, p1, p2, u, 0, row0, u, 1, row0, bias[0:1], seq, nseq)
    p1, p2 = _hy_fwd(ctab, stab, z2, 0, 0, gc, gs, 1, seq, nseq)
    return _hy_inv(cttab, sttab, p1, p2, z2, 0, 0, u, 2, row0, bias[1:2], seq, nseq)


def _chunk_index(b, c, rev, n_ctx_chunks, n_lat_chunks, bsz):
    in_ctx = c < n_ctx_chunks
    if rev:
        cc = n_ctx_chunks - 1 - c
        lc = n_lat_chunks - 1 - (c - n_ctx_chunks)
    else:
        cc = c
        lc = c - n_ctx_chunks
    return jnp.where(in_ctx, bsz * n_lat_chunks + b * n_ctx_chunks + cc, b * n_lat_chunks + lc)


def _tri(n, rev):
    r = lax.broadcasted_iota(I32, (n, n), 0)
    c = lax.broadcasted_iota(I32, (n, n), 1)
    return (c >= r) if rev else (c <= r)


def _mlstm_body(qk_ref, v_ref, gcol_ref, grow_ref, gbc_ref, gbr_ref, h_ref, ct_scr, n_scr, m_scr, *, rev):
    T = ML_CHUNK

    @pl.when(pl.program_id(1) == 0)
    def _():
        ct_scr[...] = jnp.zeros_like(ct_scr)
        n_scr[...] = jnp.zeros_like(n_scr)
        m_scr[...] = jnp.zeros_like(m_scr)

    mask = _tri(T, rev)
    m01 = jnp.where(mask, 1.0, 0.0).astype(BF16)
    m01t = jnp.where(_tri(T, not rev), 1.0, 0.0).astype(BF16)
    gcol = gcol_ref[:, 0:16] + gbc_ref[...]
    grow = grow_ref[...] + gbr_ref[...]
    cum_col = _dot_sel(m01, _log_sigmoid(gcol))
    cum_row = _sel_dot(_log_sigmoid(grow), m01t)
    last = 0 if rev else T - 1
    d = 2 if rev else 0
    for h in range(ML_HEADS):
        ci, cf = d * ML_HEADS + h, (d + 1) * ML_HEADS + h
        ig_col, ig_row = gcol[:, ci:ci + 1], grow[ci:ci + 1, :]
        b_col, b_row = cum_col[:, cf:cf + 1], cum_row[cf:cf + 1, :]
        m = m_scr[h:h + 1, 0:1]
        a_col = b_col + m
        dmat = jnp.where(mask, b_col - b_row + ig_row, -jnp.inf)
        mt = jnp.maximum(a_col, jnp.max(dmat, axis=1, keepdims=True))
        inter = jnp.exp(a_col - mt)
        q = qk_ref[:, h * ML_HD:(h + 1) * ML_HD]
        k = qk_ref[:, W + h * ML_HD:W + (h + 1) * ML_HD]
        v = v_ref[:, h * ML_HD:(h + 1) * ML_HD]
        qb, kb = q.astype(BF16), k.astype(BF16)
        s = _dot_nt(qb, kb) * jnp.exp(dmat - mt)
        ct = ct_scr[h]
        nrow = n_scr[h:h + 1, :]
        num = _dot(s.astype(BF16), v.astype(BF16)) + inter * _dot(qb, ct.astype(BF16))
        den = jnp.sum(s, axis=1, keepdims=True) + inter * jnp.sum(q * nrow, axis=1, keepdims=True)
        h_ref[:, h * ML_HD:(h + 1) * ML_HD] = num / jnp.maximum(jnp.abs(den), jnp.exp(-mt))
        m_new = mt[last:last + 1, :]
        tot = b_col[last:last + 1, :]
        ws = jnp.exp(tot - b_col + ig_col - m_new)
        dec = jnp.exp(tot + m - m_new)
        ct_scr[h] = dec * ct + _dot(k.T.astype(BF16), (v * ws).astype(BF16))
        n_scr[h:h + 1, :] = dec * nrow + jnp.sum(k * ws, axis=0, keepdims=True)
        m_scr[h:h + 1, :] = jnp.broadcast_to(m_new, (1, LANES))


def _mlstm_scan(qk, p, gcol_src, grow, gate_b, rev, bsz, seq, ctx_len):
    T = ML_CHUNK
    rows = qk.shape[0]
    ncc, nlc = ctx_len // T, seq // T
    cidx = functools.partial(_chunk_index, rev=rev, n_ctx_chunks=ncc, n_lat_chunks=nlc, bsz=bsz)
    gbc = gate_b.reshape(1, 16)
    gbr = gate_b.reshape(16, 1)
    return pl.pallas_call(
        functools.partial(_mlstm_body, rev=rev),
        grid=(bsz, ncc + nlc),
        in_specs=[
            pl.BlockSpec((T, 2 * W), lambda b, c: (cidx(b, c), 0)),
            pl.BlockSpec((T, W), lambda b, c: (cidx(b, c), P_MLV // W)),
            pl.BlockSpec((T, LANES), lambda b, c: (cidx(b, c), P_SMALL // LANES)),
            pl.BlockSpec((None, 16, T), lambda b, c: (cidx(b, c), 0, 0)),
            pl.BlockSpec((1, 16), lambda b, c: (0, 0)),
            pl.BlockSpec((16, 1), lambda b, c: (0, 0)),
        ],
        out_specs=pl.BlockSpec((T, W), lambda b, c: (cidx(b, c), 0)),
        out_shape=jax.ShapeDtypeStruct((rows, W), F32),
        scratch_shapes=[pltpu.VMEM((ML_HEADS, ML_HD, ML_HD), F32), pltpu.VMEM((8, LANES), F32),
                        pltpu.VMEM((8, LANES), F32)],
        compiler_params=_cparams("parallel", "arbitrary"),
    )(qk, p, gcol_src, grow, gbc, gbr)


def _mlstm_fin_body(hf_ref, hb_ref, o_ref, g_ref, y_ref):
    h = hf_ref[...] + hb_ref[...]
    for i in range(ML_HEADS):
        hh = h[:, i * ML_HD:(i + 1) * ML_HD]
        hh = hh * lax.rsqrt(jnp.mean(hh * hh, axis=-1, keepdims=True) + EPS)
        sl = slice(i * ML_HD, (i + 1) * ML_HD)
        y_ref[:, sl] = hh * g_ref[:, sl] * _sigmoid(o_ref[:, sl])


def _mlstm_finish(hf, hb, p, norm_g):
    rows = hf.shape[0]
    tm = _tile(rows, 512)
    return pl.pallas_call(
        _mlstm_fin_body,
        grid=(rows // tm,),
        in_specs=[pl.BlockSpec((tm, W), lambda i: (i, 0)), pl.BlockSpec((tm, W), lambda i: (i, 0)),
                  pl.BlockSpec((tm, W), lambda i: (i, P_MLO // W)), pl.BlockSpec((1, W), lambda i: (0, 0))],
        out_specs=pl.BlockSpec((tm, W), lambda i: (i, 0)),
        out_shape=jax.ShapeDtypeStruct((rows, W), F32),
        compiler_params=_cparams("parallel"),
    )(hf, hb, p, norm_g)


def _ssd_body(xbc_ref, dcol_ref, drow_ref, dbc_ref, dbr_ref, ac_ref, ar_ref, y_ref, st_scr, *, rev):
    T = SSD_CHUNK

    @pl.when(pl.program_id(1) == 0)
    def _():
        st_scr[...] = jnp.zeros_like(st_scr)

    mask = _tri(T, rev)
    m01 = jnp.where(mask, 1.0, 0.0).astype(BF16)
    m01t = jnp.where(_tri(T, not rev), 1.0, 0.0).astype(BF16)
    dt_col = _softplus(dcol_ref[:, 16:32] + dbc_ref[...])
    dt_row = _softplus(drow_ref[...] + dbr_ref[...])
    acs_col = _dot_sel(m01, dt_col * ac_ref[...])
    acs_row = _sel_dot(dt_row * ar_ref[...], m01t)
    last = 0 if rev else T - 1
    d = 1 if rev else 0
    for g in range(SSD_GROUPS):
        bm = xbc_ref[:, W + g * SSD_STATE:W + (g + 1) * SSD_STATE]
        cm = xbc_ref[:, W + (SSD_GROUPS + g) * SSD_STATE:W + (SSD_GROUPS + g + 1) * SSD_STATE]
        bmb, cmb = bm.astype(BF16), cm.astype(BF16)
        cb = _dot_nt(cmb, bmb)
        bmt = bm.T.astype(BF16)
        for hh in range(SSD_HEADS // SSD_GROUPS):
            h = g * (SSD_HEADS // SSD_GROUPS) + hh
            ci = d * SSD_HEADS + h
            a_col, a_row = acs_col[:, ci:ci + 1], acs_row[ci:ci + 1, :]
            x = xbc_ref[:, h * SSD_HD:(h + 1) * SSD_HD] * dt_col[:, ci:ci + 1]
            lm = jnp.where(mask, jnp.exp(jnp.where(mask, a_col - a_row, 0.0)), 0.0)
            st = st_scr[h]
            y = _dot((cb * lm).astype(BF16), x.astype(BF16)) + _dot(cmb, st.astype(BF16)) * jnp.exp(a_col)
            y_ref[:, h * SSD_HD:(h + 1) * SSD_HD] = y
            tot = a_col[last:last + 1, :]
            xd = x * jnp.exp(tot - a_col)
            st_scr[h] = jnp.exp(tot) * st + _dot(bmt, xd.astype(BF16))


def _ssd_scan(xbc, p, drow, dt_bias, a_log, rev, bsz, seq, ctx_len):
    T = SSD_CHUNK
    rows = xbc.shape[0]
    ncc, nlc = ctx_len // T, seq // T
    cidx = functools.partial(_chunk_index, rev=rev, n_ctx_chunks=ncc, n_lat_chunks=nlc, bsz=bsz)
    a = -jnp.exp(a_log.astype(F32))
    small = lambda shape: pl.BlockSpec(shape, lambda b, c: (0, 0))
    return pl.pallas_call(
        functools.partial(_ssd_body, rev=rev),
        grid=(bsz, ncc + nlc),
        in_specs=[
            pl.BlockSpec((T, 2 * W), lambda b, c: (cidx(b, c), 0)),
            pl.BlockSpec((T, LANES), lambda b, c: (cidx(b, c), P_SMALL // LANES)),
            pl.BlockSpec((None, 16, T), lambda b, c: (cidx(b, c), 0, 0)),
            small((1, 16)), small((16, 1)), small((1, 16)), small((16, 1)),
        ],
        out_specs=pl.BlockSpec((T, W), lambda b, c: (cidx(b, c), 0)),
        out_shape=jax.ShapeDtypeStruct((rows, W), F32),
        scratch_shapes=[pltpu.VMEM((SSD_HEADS, SSD_STATE, SSD_HD), F32)],
        compiler_params=_cparams("parallel", "arbitrary"),
    )(xbc, p, drow, dt_bias.reshape(1, 16), dt_bias.reshape(16, 1), a.reshape(1, 16), a.reshape(16, 1))


def _ssd_fin_body(yf_ref, yb_ref, x_ref, z_ref, dsk_ref, g_ref, o_ref):
    y = yf_ref[...] + yb_ref[...] + dsk_ref[...] * x_ref[...]
    y = y * _silu(z_ref[...])
    o_ref[...] = y * lax.rsqrt(jnp.mean(y * y, axis=-1, keepdims=True) + EPS) * g_ref[...]


def _ssd_finish(yf, yb, xbc, p, d_skip, norm_g):
    rows = yf.shape[0]
    tm = _tile(rows, 512)
    blk = lambda col: pl.BlockSpec((tm, W), lambda i: (i, col))
    vec = pl.BlockSpec((1, W), lambda i: (0, 0))
    return pl.pallas_call(
        _ssd_fin_body,
        grid=(rows // tm,),
        in_specs=[blk(0), blk(0), blk(0), blk(P_SSDZ // W), vec, vec],
        out_specs=blk(0),
        out_shape=jax.ShapeDtypeStruct((rows, W), F32),
        compiler_params=_cparams("parallel"),
    )(yf, yb, xbc, p, jnp.repeat(d_skip, SSD_HD).reshape(1, W), norm_g)


def _s5_disc_body(are_ref, aim_ref, ldt_ref, bre_ref, bim_ref, abre_ref, abim_ref, bbre_ref, bbim_ref):
    lam_re = jnp.minimum(are_ref[...], -1e-4)
    a_im = aim_ref[...]
    dt = jnp.exp(ldt_ref[...])
    mag = jnp.exp(lam_re * dt)
    ab_re, ab_im = mag * jnp.cos(a_im * dt), mag * jnp.sin(a_im * dt)
    den = lam_re * lam_re + a_im * a_im
    nr, ni = ab_re - 1.0, ab_im
    f_re = (nr * lam_re + ni * a_im) / den
    f_im = (ni * lam_re - nr * a_im) / den
    abre_ref[...] = ab_re
    abim_ref[...] = ab_im
    bbre_ref[...] = f_re * bre_ref[...] - f_im * bim_ref[...]
    bbim_ref[...] = f_re * bim_ref[...] + f_im * bre_ref[...]


def _s5_discretise(a_re, a_im, log_dt, b_re, b_im):
    gn = S5_NS
    col = lambda v: v.reshape(gn, 1)
    ldt = jnp.repeat(log_dt, S5_STATE).reshape(gn, 1)
    mat = lambda v: v.reshape(gn, S5_GROUP)
    cs = pl.BlockSpec((gn, 1), lambda: (0, 0))
    ms = pl.BlockSpec((gn, S5_GROUP), lambda: (0, 0))
    return pl.pallas_call(
        _s5_disc_body,
        in_specs=[cs, cs, cs, ms, ms],
        out_specs=[cs, cs, ms, ms],
        out_shape=[jax.ShapeDtypeStruct((gn, 1), F32)] * 2 + [jax.ShapeDtypeStruct((gn, S5_GROUP), F32)] * 2,
    )(col(a_re), col(a_im), ldt, mat(b_re), mat(b_im))


def _s5_body(lhs_ref, wb_ref, a_ref, wc_ref, y_ref, x_scr, st_scr, *, nrow, steps):
    @pl.when(pl.program_id(0) == 0)
    def _():
        st_scr[...] = jnp.zeros_like(st_scr)

    x_scr[...] = _dot(lhs_ref[...], wb_ref[...])
    cg = 512
    for g in range(S5_NS // cg):
        re = slice(g * cg, (g + 1) * cg)
        im = slice(S5_NS + g * cg, S5_NS + (g + 1) * cg)
        ar, ai = a_ref[:, re], a_ref[:, im]

        def step(j, carry):
            sr, si = carry
            r0 = pl.multiple_of(j * nrow, nrow)
            nr = ar * sr - ai * si + x_scr[pl.ds(r0, nrow), re]
            ni = ar * si + ai * sr + x_scr[pl.ds(r0, nrow), im]
            x_scr[pl.ds(r0, nrow), re] = nr
            x_scr[pl.ds(r0, nrow), im] = ni
            return nr, ni

        sr, si = lax.fori_loop(0, steps, step, (st_scr[:, re], st_scr[:, im]))
        st_scr[:, re] = sr
        st_scr[:, im] = si
    y_ref[...] = _dot(x_scr[...].astype(BF16), wc_ref[...])


def _s5_scan(lhs, wb, atab, wc, nrow, steps):
    rows = lhs.shape[0]
    tr = nrow * steps
    return pl.pallas_call(
        functools.partial(_s5_body, nrow=nrow, steps=steps),
        grid=(rows // tr,),
        in_specs=[
            pl.BlockSpec((tr, 2 * W), lambda i: (i, 0)),
            pl.BlockSpec((2 * W, 2 * S5_NS), lambda i: (0, 0)),
            pl.BlockSpec((nrow, 2 * S5_NS), lambda i: (0, 0)),
            pl.BlockSpec((2 * S5_NS, W), lambda i: (0, 0)),
        ],
        out_specs=pl.BlockSpec((tr, W), lambda i: (i, 0)),
        out_shape=jax.ShapeDtypeStruct((rows, W), F32),
        scratch_shapes=[pltpu.VMEM((tr, 2 * S5_NS), F32), pltpu.VMEM((nrow, 2 * S5_NS), F32)],
        compiler_params=_cparams("arbitrary"),
    )(lhs, wb, atab, wc)


def _s5_fin_body(yf_ref, yb_ref, u_ref, d_ref, w_ref, b_ref, o_ref):
    y = yf_ref[...] + yb_ref[...] + d_ref[...] * u_ref[...]
    g = _gelu_tanh(y)
    o_ref[...] = g * _sigmoid(_dot(g.astype(BF16), w_ref[...].astype(BF16)) + b_ref[...])


def _s5_finish(yf, yb, p, d_skip, glu_w, glu_b):
    rows = yf.shape[0]
    tm = _tile(rows, 512)
    blk = lambda col: pl.BlockSpec((tm, W), lambda i: (i, col))
    vec = pl.BlockSpec((1, W), lambda i: (0, 0))
    return pl.pallas_call(
        _s5_fin_body,
        grid=(rows // tm,),
        in_specs=[blk(0), blk(0), blk(P_S5 // W), vec, pl.BlockSpec((W, W), lambda i: (0, 0)), vec],
        out_specs=blk(0),
        out_shape=jax.ShapeDtypeStruct((rows, W), F32),
        compiler_params=_cparams("parallel"),
    )(yf, yb, p, d_skip.reshape(1, W), glu_w, glu_b.reshape(1, W))


def _block_diag(m):
    g, a, b = m.shape
    eye = jnp.eye(g, dtype=m.dtype)
    return (eye[:, None, :, None] * m[:, :, None, :]).reshape(g * a, g * b)


def _s5_branch(p, n_lat_rows, bsz, seq, ctx_len, a_re, a_im, log_dt, b_re, b_im, c_re, c_im, d_skip, glu_w, glu_b):
    disc = [_s5_discretise(a_re[d], a_im[d], log_dt[d], b_re, b_im) for d in range(2)]
    wb = []
    for d in range(2):
        _, _, bb_re, bb_im = disc[d]
        tr = lambda v: _block_diag(v.reshape(S5_GROUPS, S5_STATE, S5_GROUP).transpose(0, 2, 1))
        wb.append(jnp.concatenate([tr(bb_re), tr(bb_im)], axis=1))
    wb = jnp.concatenate(wb, axis=0).astype(BF16)
    arow = lambda d: jnp.concatenate([disc[d][0].reshape(1, S5_NS), disc[d][1].reshape(1, S5_NS)], axis=1)
    atab = jnp.concatenate([jnp.broadcast_to(arow(0), (bsz, 2 * S5_NS)),
                            jnp.broadcast_to(arow(1), (bsz, 2 * S5_NS))], axis=0)
    cre = _block_diag(c_re.transpose(0, 2, 1))
    cim = _block_diag(c_im.transpose(0, 2, 1))
    wc = jnp.concatenate([cre, -cim], axis=0).astype(BF16)
    u = p[:, P_S5:P_S5 + W]
    ul = u[:n_lat_rows].reshape(bsz, seq, W)
    uc = u[n_lat_rows:].reshape(bsz, ctx_len, W)
    fwd = jnp.concatenate([uc, ul], axis=1)
    bwd = jnp.concatenate([uc[:, ::-1], ul[:, ::-1]], axis=1)
    zeros = jnp.zeros_like(fwd)
    lhs = jnp.concatenate([jnp.concatenate([fwd, zeros], axis=-1), jnp.concatenate([zeros, bwd], axis=-1)], axis=0)
    npos = seq + ctx_len
    lhs = lhs.transpose(1, 0, 2).reshape(npos * 2 * bsz, 2 * W).astype(BF16)
    y = _s5_scan(lhs, wb, atab, wc, 2 * bsz, _tile(npos, 64))
    y = y.reshape(npos, 2, bsz, W).transpose(1, 2, 0, 3)
    yf, yb = y[0], y[1]
    order = lambda lat, ctx: jnp.concatenate([lat.reshape(bsz * seq, W), ctx.reshape(bsz * ctx_len, W)], axis=0)
    yf = order(yf[:, ctx_len:], yf[:, :ctx_len])
    yb = order(yb[:, ctx_len:][:, ::-1], yb[:, :ctx_len][:, ::-1])
    return _s5_finish(yf, yb, p, d_skip, glu_w, glu_b)


def _merge_body(y0, y1, y2, y3, g0, g1, g2, g3, w0, w1, w2, w3, o_ref):
    acc = None
    for y, g, w in ((y0, g0, w0), (y1, g1, w1), (y2, g2, w2), (y3, g3, w3)):
        term = _sigmoid(g[...]) * _dot(y[...].astype(BF16), w[...])
        acc = term if acc is None else acc + term
    o_ref[...] = acc.astype(BF16)


def _merge(ys, p, wb):
    rows = p.shape[0]
    tm, tn = _tile(rows, 512), 512
    ysp = pl.BlockSpec((tm, W), lambda i, j: (i, 0))
    gsp = lambda k: pl.BlockSpec((tm, tn), lambda i, j: (i, (P_GATE + k * D_MODEL) // tn + j))
    wsp = lambda k: pl.BlockSpec((None, W, tn), lambda i, j: (k, 0, j))
    return pl.pallas_call(
        _merge_body,
        grid=(rows // tm, D_MODEL // tn),
        in_specs=[ysp] * 4 + [gsp(k) for k in range(4)] + [wsp(k) for k in range(4)],
        out_specs=pl.BlockSpec((tm, tn), lambda i, j: (i, j)),
        out_shape=jax.ShapeDtypeStruct((rows, D_MODEL), BF16),
        compiler_params=_cparams("parallel", "parallel"),
    )(*ys, p, p, p, p, wb, wb, wb, wb)


def _outproj_body(a_ref, w_ref, xl_ref, xc_ref, mod_ref, o_ref, *, n_lat_tiles):
    y = mod_ref[2:3, :] * _dot(a_ref[...], w_ref[...])
    i = pl.program_id(0)

    @pl.when(i < n_lat_tiles)
    def _():
        o_ref[...] = xl_ref[...] + y

    @pl.when(i >= n_lat_tiles)
    def _():
        o_ref[...] = xc_ref[...] + y


def _outproj(acc, w, xl, xc, mods, seq, bsz):
    nl, nc = xl.shape[0], xc.shape[0]
    tm = _tile(math.gcd(nl, nc), 512)
    tn = 512
    n_lat = nl // tm
    return pl.pallas_call(
        functools.partial(_outproj_body, n_lat_tiles=n_lat),
        grid=((nl + nc) // tm, D_MODEL // tn),
        in_specs=[
            pl.BlockSpec((tm, D_MODEL), lambda i, j: (i, 0)),
            pl.BlockSpec((D_MODEL, tn), lambda i, j: (0, j)),
            pl.BlockSpec((tm, tn), lambda i, j: (jnp.minimum(i, n_lat - 1), j)),
            pl.BlockSpec((tm, tn), lambda i, j: (jnp.maximum(i - n_lat, 0), j)),
            pl.BlockSpec((None, 6, tn), lambda i, j: (_mod_index(i, tm, n_lat, seq, bsz), 0, j)),
        ],
        out_specs=pl.BlockSpec((tm, tn), lambda i, j: (i, j)),
        out_shape=jax.ShapeDtypeStruct((nl + nc, D_MODEL), F32),
        compiler_params=_cparams("parallel", "parallel"),
    )(acc, w, xl, xc, mods)


def _router_body(x_ref, g_ref, mod_ref, wr_ref, h_ref, p_ref):
    h = _norm_mod(x_ref[...], g_ref[...], mod_ref[3:4, :], mod_ref[4:5, :])
    h_ref[...] = h.astype(BF16)
    logits = _dot3(h, wr_ref[...])
    lane = lax.broadcasted_iota(I32, logits.shape, 1)
    logits = jnp.where(lane < N_EXPERTS, logits, -jnp.inf)
    e = jnp.exp(logits - jnp.max(logits, axis=-1, keepdims=True))
    p_ref[...] = e / jnp.sum(e, axis=-1, keepdims=True)


def _router(xm, g, mods, w_router, n_lat_rows, seq, bsz):
    rows = xm.shape[0]
    tm = _tile(math.gcd(n_lat_rows, rows - n_lat_rows), 512)
    n_lat = n_lat_rows // tm
    wr = jnp.pad(w_router, ((0, 0), (0, LANES - N_EXPERTS)))
    return pl.pallas_call(
        _router_body,
        grid=(rows // tm,),
        in_specs=[
            pl.BlockSpec((tm, D_MODEL), lambda i: (i, 0)),
            pl.BlockSpec((1, D_MODEL), lambda i: (0, 0)),
            pl.BlockSpec((None, 6, D_MODEL), lambda i: (_mod_index(i, tm, n_lat, seq, bsz), 0, 0)),
            pl.BlockSpec((D_MODEL, LANES), lambda i: (0, 0)),
        ],
        out_specs=[pl.BlockSpec((tm, D_MODEL), lambda i: (i, 0)), pl.BlockSpec((tm, LANES), lambda i: (i, 0))],
        out_shape=[jax.ShapeDtypeStruct((rows, D_MODEL), BF16), jax.ShapeDtypeStruct((rows, LANES), F32)],
        compiler_params=_cparams("parallel"),
    )(xm, g, mods, wr)


def _topk_body(p_ref, o_ref, *, n, cap):
    bits = pltpu.bitcast(p_ref[...], I32)
    capf = float(cap)

    def count(mask):
        return jnp.sum(jnp.where(mask, 1.0, 0.0), axis=0, keepdims=True)

    def vstep(i, thr):
        cand = thr | lax.shift_left(jnp.int32(1), 30 - i)
        return jnp.where(count(bits >= cand) >= capf, cand, thr)

    thr = lax.fori_loop(0, 31, vstep, jnp.zeros((1, LANES), I32))
    gt = bits > thr
    eq = bits == thr
    need = capf - count(gt)
    t = lax.broadcasted_iota(I32, bits.shape, 0)
    nbits = max(1, (n - 1).bit_length())

    def istep(i, j):
        cand = j + lax.shift_left(jnp.int32(1), nbits - 1 - i)
        return jnp.where(count(eq & (t < cand)) < need, cand, j)

    jmax = lax.fori_loop(0, nbits, istep, jnp.zeros((1, LANES), I32))
    sel = gt | (eq & (t <= jmax))
    self32 = jnp.where(sel, 1.0, 0.0)
    blk = min(n, 256)
    r = lax.broadcasted_iota(I32, (blk, blk), 0)
    c = lax.broadcasted_iota(I32, (blk, blk), 1)
    lower = jnp.where(c < r, 1.0, 0.0).astype(BF16)
    carry = jnp.zeros((1, LANES), F32)
    for i in range(n // blk):
        sb = self32[i * blk:(i + 1) * blk]
        rank = _dot(lower, sb.astype(BF16)) + carry
        o_ref[i * blk:(i + 1) * blk, :] = jnp.where(sb > 0.0, rank, -1.0).astype(I32)
        carry = carry + jnp.sum(sb, axis=0, keepdims=True)


def _topk(probs, row0, n, nsets):
    cap = EC_CAPACITY * n // N_EXPERTS
    return pl.pallas_call(
        functools.partial(_topk_body, n=n, cap=cap),
        grid=(nsets,),
        in_specs=[pl.BlockSpec((n, LANES), lambda s: (row0 // n + s, 0))],
        out_specs=pl.BlockSpec((n, LANES), lambda s: (s, 0)),
        out_shape=jax.ShapeDtypeStruct((nsets * n, LANES), I32),
        compiler_params=_cparams("parallel"),
    )(probs)


def _gather_body(slot_ref, p_ref, h_ref, xs_ref, gate_ref, *, cap):
    e = pl.program_id(2)
    slot = slot_ref[pl.ds(e, 1), :]
    r = lax.broadcasted_iota(I32, (cap, slot.shape[1]), 0)
    onehot = jnp.where(slot == r, 1.0, 0.0).astype(BF16)
    xs_ref[...] = _dot(onehot, h_ref[...]).astype(BF16)
    g = _dot_sel(onehot, p_ref[...])
    lane = lax.broadcasted_iota(I32, g.shape, 1)
    gate_ref[...] = jnp.sum(jnp.where(lane == e, g, 0.0), axis=1, keepdims=True)


def _gather(slot_row, probs, h2, row0, n, nsets):
    cap = EC_CAPACITY * n // N_EXPERTS
    dh = D_MODEL // 2
    xs, gate = pl.pallas_call(
        functools.partial(_gather_body, cap=cap),
        grid=(nsets, 2, N_EXPERTS),
        in_specs=[
            pl.BlockSpec((None, N_EXPERTS, n), lambda s, k, e: (s, 0, 0)),
            pl.BlockSpec((n, LANES), lambda s, k, e: (row0 // n + s, 0)),
            pl.BlockSpec((n, dh), lambda s, k, e: (row0 // n + s, k)),
        ],
        out_specs=[pl.BlockSpec((None, cap, dh), lambda s, k, e: (e, s, k)),
                   pl.BlockSpec((None, None, cap, 1), lambda s, k, e: (k, e, s, 0))],
        out_shape=[jax.ShapeDtypeStruct((N_EXPERTS, nsets * cap, D_MODEL), BF16),
                   jax.ShapeDtypeStruct((2, N_EXPERTS, nsets * cap, 1), F32)],
        compiler_params=_cparams("parallel", "arbitrary", "arbitrary"),
    )(slot_row, probs, h2)
    return xs, gate[0]


def _ffn_body(*refs, with_ctx, ml):
    if with_ctx:
        xl_ref, gl_ref, xc_ref, gc_ref, wg_ref, wu_ref, wd_ref, yl_ref, yc_ref, acc = refs
        xs = jnp.concatenate([xl_ref[...], xc_ref[...]], axis=0)
    else:
        xl_ref, gl_ref, wg_ref, wu_ref, wd_ref, yl_ref, acc = refs
        xs = xl_ref[...]
    f = pl.program_id(2)

    @pl.when(f == 0)
    def _():
        acc[...] = jnp.zeros_like(acc)

    hid = _silu(_dot(xs, wg_ref[...].astype(BF16))) * _dot(xs, wu_ref[...].astype(BF16))
    acc[...] += _dot(hid.astype(BF16), wd_ref[...].astype(BF16))

    @pl.when(f == pl.num_programs(2) - 1)
    def _():
        yl_ref[...] = (acc[0:ml, :] * gl_ref[...]).astype(BF16)
        if with_ctx:
            yc_ref[...] = (acc[ml:, :] * gc_ref[...]).astype(BF16)


def _ffn(xs_l, g_l, xs_c, g_c, w_gate, w_up, w_down):
    with_ctx = xs_c is not None
    nsplit = 2
    ml = xs_l.shape[1] // nsplit
    mc = xs_c.shape[1] // nsplit if with_ctx else 0
    fc = 256
    row = lambda m, last: pl.BlockSpec((None, m, last), lambda e, s, f: (e, s, 0))
    in_specs = [row(ml, D_MODEL), row(ml, 1)]
    args = [xs_l, g_l]
    out_specs = [row(ml, D_MODEL)]
    out_shape = [jax.ShapeDtypeStruct(xs_l.shape, BF16)]
    if with_ctx:
        in_specs += [row(mc, D_MODEL), row(mc, 1)]
        args += [xs_c, g_c]
        out_specs.append(row(mc, D_MODEL))
        out_shape.append(jax.ShapeDtypeStruct(xs_c.shape, BF16))
    in_specs += [pl.BlockSpec((None, D_MODEL, fc), lambda e, s, f: (e, 0, f)),
                 pl.BlockSpec((None, D_MODEL, fc), lambda e, s, f: (e, 0, f)),
                 pl.BlockSpec((None, fc, D_MODEL), lambda e, s, f: (e, f, 0))]
    args += [w_gate, w_up, w_down]
    out = pl.pallas_call(
        functools.partial(_ffn_body, with_ctx=with_ctx, ml=ml),
        grid=(N_EXPERTS, nsplit, D_EXPERT // fc),
        in_specs=in_specs,
        out_specs=out_specs,
        out_shape=out_shape,
        scratch_shapes=[pltpu.VMEM((ml + mc, D_MODEL), F32)],
        compiler_params=_cparams("parallel", "parallel", "arbitrary"),
    )(*args)
    return (out[0], out[1]) if with_ctx else (out[0], None)


def _scatter_body(slot_ref, y_ref, x_ref, mod_ref, o_ref, acc, *, cap):
    e = pl.program_id(2)

    @pl.when(e == 0)
    def _():
        acc[...] = jnp.zeros_like(acc)

    slot = slot_ref[...]
    lane = lax.broadcasted_iota(I32, slot.shape, 1)
    col = jnp.sum(jnp.where(lane == e, slot, 0).astype(F32), axis=1, keepdims=True)
    r = lax.broadcasted_iota(I32, (slot.shape[0], cap), 1).astype(F32)
    onehot = jnp.where(col == r, 1.0, 0.0).astype(BF16)
    acc[...] += _dot(onehot, y_ref[...])

    @pl.when(e == pl.num_programs(2) - 1)
    def _():
        o_ref[...] = x_ref[...] + mod_ref[5:6, :] * acc[...]


def _scatter(slot_col, y, xm, mods, row0, n, nsets, mod_of_set):
    cap = EC_CAPACITY * n // N_EXPERTS
    tt = _tile(n, 1024)
    nt = n // tt
    return pl.pallas_call(
        functools.partial(_scatter_body, cap=cap),
        grid=(nsets, nt, N_EXPERTS),
        in_specs=[
            pl.BlockSpec((tt, LANES), lambda s, t, e: (s * nt + t, 0)),
            pl.BlockSpec((None, cap, D_MODEL), lambda s, t, e: (e, s, 0)),
            pl.BlockSpec((tt, D_MODEL), lambda s, t, e: (row0 // tt + s * nt + t, 0)),
            pl.BlockSpec((None, 6, D_MODEL), lambda s, t, e: (mod_of_set(s), 0, 0)),
        ],
        out_specs=pl.BlockSpec((tt, D_MODEL), lambda s, t, e: (s * nt + t, 0)),
        out_shape=jax.ShapeDtypeStruct((nsets * n, D_MODEL), F32),
        scratch_shapes=[pltpu.VMEM((tt, D_MODEL), F32)],
        compiler_params=_cparams("parallel", "parallel", "arbitrary"),
    )(slot_col, y, xm, mods)


def _moe(xm, g, mods, w_router, w_gate, w_up, w_down, n_lat_rows, bsz, seq, ctx_len, with_ctx):
    h2, probs = _router(xm, g, mods, w_router, n_lat_rows, seq, bsz)

    def route(row0, n):
        slot_col = _topk(probs, row0, n, bsz)
        slot_row = slot_col.reshape(bsz, n, LANES)[:, :, :N_EXPERTS].transpose(0, 2, 1)
        xs, gate = _gather(slot_row, probs, h2, row0, n, bsz)
        return slot_col, xs, gate

    slot_l, xs_l, gate_l = route(0, seq)
    if with_ctx:
        slot_c, xs_c, gate_c = route(n_lat_rows, ctx_len)
    else:
        xs_c = gate_c = None
    y_l, y_c = _ffn(xs_l, gate_l, xs_c, gate_c, w_gate, w_up, w_down)
    xl = _scatter(slot_l, y_l, xm, mods, 0, seq, bsz, lambda s: s)
    xc = _scatter(slot_c, y_c, xm, mods, n_lat_rows, ctx_len, bsz, lambda s: bsz) if with_ctx else None
    return xl, xc


def _final_norm_body(x_ref, g_ref, o_ref):
    x = x_ref[...]
    o_ref[...] = x * lax.rsqrt(jnp.mean(x * x, axis=-1, keepdims=True) + EPS) * g_ref[...]


def _final_norm(x, g):
    rows, d = x.shape
    tm = _tile(rows, 512)
    return pl.pallas_call(
        _final_norm_body,
        grid=(rows // tm,),
        in_specs=[pl.BlockSpec((tm, d), lambda i: (i, 0)), pl.BlockSpec((1, d), lambda i: (0, 0))],
        out_specs=pl.BlockSpec((tm, d), lambda i: (i, 0)),
        out_shape=jax.ShapeDtypeStruct((rows, d), F32),
        compiler_params=_cparams("parallel"),
    )(x, g)


def _reorder_w_in(w):
    pad = jnp.zeros((w.shape[0], P_WIDTH - P_SMALL - 32), w.dtype)
    return jnp.concatenate([w[:, 0:3584], w[:, 3600:5648], w[:, 5664:13856], w[:, 3584:3600], w[:, 5648:5664], pad],
                           axis=1).astype(BF16)


def _chunk_rows(a, t):
    return a.reshape(a.shape[0] // t, t, a.shape[1]).transpose(0, 2, 1)


def _mixer(xl, xc, mods, lw, tabs_lat, tabs_ctx, bsz, seq, ctx_len, with_ctx_out):
    n_lat_rows = bsz * seq
    p = _inproj(xl, xc, lw["norm_mix"], mods, lw["w_in"], seq, bsz)
    ones = lambda n: jnp.ones((1, n), F32)
    u_hy = _short_conv(p, P_HY, 3 * W, lw["hy_conv_w"], lw["hy_conv_b"].reshape(1, -1), ones(3 * W), False,
                       n_lat_rows, ctx_len)
    hy_args = (lw["hy_w1"], lw["hy_b1"], lw["hy_fr1"], lw["hy_w2"], lw["hy_b2"], lw["hy_fr2"], lw["hy_w3"],
               lw["hy_decay"])

    def hy_filter(n, tabs):
        gp, gm = _hy_filters(n, *hy_args)
        return _matmul(tabs[0], gp, 512, 512), _matmul(tabs[1], gm, 512, 512), lw["hy_bias"]

    hy_lat = _hyena(u_hy, 0, seq, bsz, tabs_lat, hy_filter(seq, tabs_lat))
    if with_ctx_out:
        hy_ctx = _hyena(u_hy, n_lat_rows, ctx_len, bsz, tabs_ctx, hy_filter(ctx_len, tabs_ctx))
    else:
        hy_ctx = jnp.zeros((bsz * ctx_len, W), F32)
    y_hy = jnp.concatenate([hy_lat, hy_ctx], axis=0)
    post = jnp.concatenate([jnp.ones((1, W), F32), jnp.full((1, W), ML_HD ** -0.5, F32)], axis=1)
    qk = _short_conv(p, P_MLQK, 2 * W, lw["ml_conv_w"], lw["ml_conv_b"].reshape(1, -1), post, True, n_lat_rows, ctx_len)
    small = p[:, P_SMALL:P_SMALL + 32]
    grow = _chunk_rows(small[:, 0:16], ML_CHUNK)
    gate_b = lw["ml_gate_b"].reshape(16)
    hf = _mlstm_scan(qk, p, p, grow, gate_b, False, bsz, seq, ctx_len)
    hb = _mlstm_scan(qk, p, p, grow, gate_b, True, bsz, seq, ctx_len)
    y_ml = _mlstm_finish(hf, hb, p, lw["ml_norm"].reshape(1, W))
    y_s5 = _s5_branch(p, n_lat_rows, bsz, seq, ctx_len, lw["s5_a_re"], lw["s5_a_im"], lw["s5_log_dt"], lw["s5_b_re"],
                      lw["s5_b_im"], lw["s5_c_re"], lw["s5_c_im"], lw["s5_d"], lw["s5_glu_w"], lw["s5_glu_b"])
    xbc = _short_conv(p, P_XBC, 2 * W, lw["ssd_conv_w"], lw["ssd_conv_b"].reshape(1, -1), ones(2 * W), True,
                      n_lat_rows, ctx_len)
    drow = _chunk_rows(small[:, 16:32], SSD_CHUNK)
    dtb = lw["ssd_dt_bias"].reshape(16)
    alog = lw["ssd_a_log"].reshape(16)
    sf = _ssd_scan(xbc, p, drow, dtb, alog, False, bsz, seq, ctx_len)
    sb = _ssd_scan(xbc, p, drow, dtb, alog, True, bsz, seq, ctx_len)
    y_ssd = _ssd_finish(sf, sb, xbc, p, lw["ssd_d"], lw["ssd_norm"].reshape(1, W))
    acc = _merge((y_hy, y_ml, y_s5, y_ssd), p, lw["w_branch"])
    return _outproj(acc, lw["w_out"], xl, xc, mods, seq, bsz)


_PER_LAYER = ("ada_w", "ada_b", "norm_mix", "norm_ffn", "w_in", "hy_conv_w", "hy_conv_b", "hy_w1", "hy_b1", "hy_fr1",
              "hy_w2", "hy_b2", "hy_fr2", "hy_w3", "hy_decay", "hy_bias", "ml_conv_w", "ml_conv_b", "ml_gate_b",
              "ml_norm", "s5_a_re", "s5_a_im", "s5_log_dt", "s5_b_re", "s5_b_im", "s5_c_re", "s5_c_im", "s5_d",
              "s5_glu_w", "s5_glu_b", "ssd_conv_w", "ssd_conv_b", "ssd_dt_bias", "ssd_a_log", "ssd_d", "ssd_norm",
              "w_branch", "w_out", "w_router", "w_gate", "w_up", "w_down")


def kernel(x, c, ctx, c_ctx, ada_w, ada_b, norm_mix, norm_ffn, w_in, hy_conv_w, hy_conv_b, hy_w1, hy_b1, hy_fr1, hy_w2, hy_b2, hy_fr2, hy_w3, hy_decay, hy_bias, ml_conv_w, ml_conv_b, ml_gate_b, ml_norm, s5_a_re, s5_a_im, s5_log_dt, s5_b_re, s5_b_im, s5_c_re, s5_c_im, s5_d, s5_glu_w, s5_glu_b, ssd_conv_w, ssd_conv_b, ssd_dt_bias, ssd_a_log, ssd_d, ssd_norm, w_branch, w_out, w_router, w_gate, w_up, w_down, final_norm):
    stacked = dict(zip(_PER_LAYER, (ada_w, ada_b, norm_mix, norm_ffn, w_in, hy_conv_w, hy_conv_b, hy_w1, hy_b1, hy_fr1,
                                    hy_w2, hy_b2, hy_fr2, hy_w3, hy_decay, hy_bias, ml_conv_w, ml_conv_b, ml_gate_b,
                                    ml_norm, s5_a_re, s5_a_im, s5_log_dt, s5_b_re, s5_b_im, s5_c_re, s5_c_im, s5_d,
                                    s5_glu_w, s5_glu_b, ssd_conv_w, ssd_conv_b, ssd_dt_bias, ssd_a_log, ssd_d, ssd_norm,
                                    w_branch, w_out, w_router, w_gate, w_up, w_down)))
    bsz, seq, d = x.shape
    ctx_len = ctx.shape[1]
    depth = ada_w.shape[0]
    assert d == D_MODEL and 2 * bsz <= 8 and seq % 256 == 0 and ctx_len % SSD_CHUNK == 0
    xl = x.reshape(bsz * seq, d)
    xc = ctx.reshape(bsz * ctx_len, d)
    cc = jnp.zeros((8, d), F32).at[:bsz].set(c).at[bsz].set(c_ctx)
    tabs_lat = _dft_tables(seq)
    tabs_ctx = _dft_tables(ctx_len)
    for i in range(depth):
        last = i == depth - 1
        lw = {k: v[i] for k, v in stacked.items()}
        lw["w_in"] = _reorder_w_in(lw["w_in"])
        lw["norm_mix"] = lw["norm_mix"].reshape(1, d)
        lw["norm_ffn"] = lw["norm_ffn"].reshape(1, d)
        lw["w_branch"] = lw["w_branch"].astype(BF16)
        lw["w_out"] = lw["w_out"].astype(BF16)
        mods = _ada(cc, lw["ada_w"], lw["ada_b"].reshape(1, -1)).reshape(8, 6, d)
        xm = _mixer(xl, xc, mods, lw, tabs_lat, tabs_ctx, bsz, seq, ctx_len, not last)
        xl, xc_new = _moe(xm, lw["norm_ffn"], mods, lw["w_router"], lw["w_gate"], lw["w_up"], lw["w_down"],
                          bsz * seq, bsz, seq, ctx_len, not last)
        if not last:
            xc = xc_new
    return _final_norm(xl, final_norm.reshape(1, d)).reshape(bsz, seq, d)
```

```python
import functools
import math

import jax
import jax.numpy as jnp
from jax import lax
from jax.experimental import pallas as pl
from jax.experimental.pallas import tpu as pltpu

F32 = jnp.float32
BF16 = jnp.bfloat16
I32 = jnp.int32

D_MODEL = 2048
W = 512
GRID_W = 64
EPS = 1e-6
HY_EMB = 33
HY_BANDS = 16
HY_FFN = 64
ML_HEADS = 4
ML_HD = 128
ML_CHUNK = 64
S5_GROUP = 16
S5_GROUPS = 32
S5_STATE = 64
S5_NS = S5_GROUPS * S5_STATE
S5_SUPER = 4
S5_T = 64
SSD_HD = 64
SSD_HEADS = 8
SSD_GROUPS = 2
SSD_STATE = 128
SSD_CHUNK = 128
N_EXPERTS = 16
EC_CAPACITY = 2
D_EXPERT = 1536
SCAN_BLOCK = 128
ROUTE_BLOCK = 256

P_HY = 0
P_MLQK = 1536
P_MLV = 2560
P_MLO = 3072
P_S5 = 3584
P_SSDZ = 4096
P_XBC = 4608
P_GATE = 5632
P_SMALL = 13824
P_WIDTH = 14336
N_SMALL = 32

LANES = 128
VMEM_LIMIT_BYTES = 56 * 1024 * 1024


def _cparams(*sem):
    return pltpu.CompilerParams(dimension_semantics=sem, vmem_limit_bytes=VMEM_LIMIT_BYTES)


def _dot(a, b):
    return jnp.dot(a, b, preferred_element_type=F32)


def _dot_nt(a, b):
    return lax.dot_general(a, b, (((1,), (1,)), ((), ())), preferred_element_type=F32)


def _split3(x):
    hi = x.astype(BF16)
    r = x - hi.astype(F32)
    mid = r.astype(BF16)
    lo = (r - mid.astype(F32)).astype(BF16)
    return hi, mid, lo


def _dot_sel(m01, x):
    hi, mid, lo = _split3(x)
    return _dot(m01, hi) + _dot(m01, mid) + _dot(m01, lo)


def _sel_dot(x, m01):
    hi, mid, lo = _split3(x)
    return _dot(hi, m01) + _dot(mid, m01) + _dot(lo, m01)


def _dot3(a, b):
    ah = a.astype(BF16)
    al = (a - ah.astype(F32)).astype(BF16)
    bh = b.astype(BF16)
    bl = (b - bh.astype(F32)).astype(BF16)
    return _dot(ah, bh) + _dot(al, bh) + _dot(ah, bl)


def _sigmoid(x):
    return 1.0 / (1.0 + jnp.exp(-x))


def _silu(x):
    return x * _sigmoid(x)


def _softplus(x):
    return jnp.maximum(x, 0.0) + jnp.log(1.0 + jnp.exp(-jnp.abs(x)))


def _log_sigmoid(x):
    return jnp.minimum(x, 0.0) - jnp.log(1.0 + jnp.exp(-jnp.abs(x)))


def _gelu_tanh(x):
    return 0.5 * x * (1.0 + jnp.tanh(math.sqrt(2.0 / math.pi) * (x + 0.044715 * (x * x * x))))


def _tile(n, pref):
    t = min(n, pref)
    while n % t:
        t //= 2
    return t


def _mod_index(i, tm, n_lat_tiles, seq, bsz):
    return jnp.where(i < n_lat_tiles, (i * tm) // seq, bsz)


def _mm_body(a_ref, b_ref, o_ref):
    o_ref[...] = _dot(a_ref[...].astype(BF16), b_ref[...].astype(BF16)).astype(o_ref.dtype)


def _matmul(a, b, tm, tn, name, out_dtype=F32):
    m, k = a.shape
    n = b.shape[1]
    tm, tn = _tile(m, tm), _tile(n, tn)
    return pl.pallas_call(
        _mm_body,
        name=name,
        grid=(m // tm, n // tn),
        in_specs=[pl.BlockSpec((tm, k), lambda i, j: (i, 0)), pl.BlockSpec((k, tn), lambda i, j: (0, j))],
        out_specs=pl.BlockSpec((tm, tn), lambda i, j: (i, j)),
        out_shape=jax.ShapeDtypeStruct((m, n), out_dtype),
        compiler_params=_cparams("parallel", "parallel"),
    )(a, b)


def _ada_body(c_ref, w_ref, b_ref, o_ref):
    c = c_ref[...]
    o_ref[...] = _dot3(_silu(c), w_ref[...]) + b_ref[...]


def _ada(cc, w, b):
    d, n = w.shape
    tn = 1024
    return pl.pallas_call(
        _ada_body,
        name="ada_mod",
        grid=(n // tn,),
        in_specs=[pl.BlockSpec((8, d), lambda j: (0, 0)), pl.BlockSpec((d, tn), lambda j: (0, j)),
                  pl.BlockSpec((1, tn), lambda j: (0, j))],
        out_specs=pl.BlockSpec((8, tn), lambda j: (0, j)),
        out_shape=jax.ShapeDtypeStruct((8, n), F32),
        compiler_params=_cparams("parallel"),
    )(cc, w, b)


def _norm_mod(x, g, shift, scale):
    y = x * lax.rsqrt(jnp.mean(x * x, axis=-1, keepdims=True) + EPS) * g
    return y * (1.0 + scale) + shift


def _normmod_body(xl_ref, xc_ref, g_ref, mod_ref, h_ref, *, n_lat_tiles):
    i = pl.program_id(0)

    @pl.when(i < n_lat_tiles)
    def _():
        h_ref[...] = _norm_mod(xl_ref[...], g_ref[...], mod_ref[0:1, :], mod_ref[1:2, :]).astype(BF16)

    @pl.when(i >= n_lat_tiles)
    def _():
        h_ref[...] = _norm_mod(xc_ref[...], g_ref[...], mod_ref[0:1, :], mod_ref[1:2, :]).astype(BF16)


def _normmod(xl, xc, g, mods, seq, bsz):
    nl, nc = xl.shape[0], xc.shape[0]
    d = xl.shape[1]
    tm = _tile(math.gcd(nl, nc), 512)
    n_lat = nl // tm
    return pl.pallas_call(
        functools.partial(_normmod_body, n_lat_tiles=n_lat),
        name="mixer_normmod",
        grid=((nl + nc) // tm,),
        in_specs=[
            pl.BlockSpec((tm, d), lambda i: (jnp.minimum(i, n_lat - 1), 0)),
            pl.BlockSpec((tm, d), lambda i: (jnp.maximum(i - n_lat, 0), 0)),
            pl.BlockSpec((1, d), lambda i: (0, 0)),
            pl.BlockSpec((None, 6, d), lambda i: (_mod_index(i, tm, n_lat, seq, bsz), 0, 0)),
        ],
        out_specs=pl.BlockSpec((tm, d), lambda i: (i, 0)),
        out_shape=jax.ShapeDtypeStruct((nl + nc, d), BF16),
        compiler_params=_cparams("parallel"),
    )(xl, xc, g, mods)


def _small_t_body(p_ref, o_ref):
    o_ref[...] = p_ref[...].T[0:N_SMALL, :]


def _small_t(p):
    rows = p.shape[0]
    tm = _tile(rows, 512)
    return pl.pallas_call(
        _small_t_body,
        name="small_transpose",
        grid=(rows // tm,),
        in_specs=[pl.BlockSpec((tm, LANES), lambda i: (i, P_SMALL // LANES))],
        out_specs=pl.BlockSpec((N_SMALL, tm), lambda i: (0, i)),
        out_shape=jax.ShapeDtypeStruct((N_SMALL, rows), F32),
        compiler_params=_cparams("parallel"),
    )(p)


def _conv_body(u_ref, w_ref, b_ref, s_ref, o_ref, *, act, n_lat_tiles, tile):
    i = pl.program_id(0)
    u = u_ref[...]
    t = lax.broadcasted_iota(I32, u.shape, 0)
    pos = jnp.where(i < n_lat_tiles, t & (GRID_W - 1), t)
    last = jnp.where(i < n_lat_tiles, GRID_W - 1, tile - 1)
    prev = jnp.where(pos == 0, 0.0, pltpu.roll(u, 1, 0))
    nxt = jnp.where(pos == last, 0.0, pltpu.roll(u, tile - 1, 0))
    y = w_ref[0:1, :] * prev + w_ref[1:2, :] * u + w_ref[2:3, :] * nxt + b_ref[...]
    if act:
        y = _silu(y)
    o_ref[...] = y * s_ref[...]


def _short_conv(p, col0, width, w, b, post, act, n_lat_rows, ctx_len, name):
    rows = p.shape[0]
    tile = ctx_len
    cb = 512
    return pl.pallas_call(
        functools.partial(_conv_body, act=act, n_lat_tiles=n_lat_rows // tile, tile=tile),
        name=name,
        grid=(rows // tile, width // cb),
        in_specs=[
            pl.BlockSpec((tile, cb), lambda i, j: (i, col0 // cb + j)),
            pl.BlockSpec((3, cb), lambda i, j: (0, j)),
            pl.BlockSpec((1, cb), lambda i, j: (0, j)),
            pl.BlockSpec((1, cb), lambda i, j: (0, j)),
        ],
        out_specs=pl.BlockSpec((tile, cb), lambda i, j: (i, j)),
        out_shape=jax.ShapeDtypeStruct((rows, width), F32),
        compiler_params=_cparams("parallel", "parallel"),
    )(p, w, b, post)


def _hy_filter_body(feat_ref, w1_ref, b1_ref, fr1_ref, w2_ref, b2_ref, fr2_ref, w3a_ref, w3b_ref,
                    da_ref, db_ref, gp_ref, gm_ref, *, seq):
    hdn = jnp.sin(fr1_ref[...] * (_dot3(feat_ref[...], w1_ref[...]) + b1_ref[...]))
    hdn = jnp.sin(fr2_ref[...] * (_dot3(hdn, w2_ref[...]) + b2_ref[...]))
    t = lax.broadcasted_iota(I32, (seq, 1), 0)
    tn = t.astype(F32) / float(seq - 1)
    hf = _dot3(hdn, w3a_ref[...]) * jnp.exp(-tn * jnp.abs(da_ref[...]))
    hb = _dot3(hdn, w3b_ref[...]) * jnp.exp(-tn * jnp.abs(db_ref[...]))
    norm = jnp.sum(jnp.abs(hf) + jnp.abs(hb), axis=0, keepdims=True)
    hf = hf / norm
    hb = jnp.where(t == 0, 0.0, hb / norm)
    gp_ref[...] = hf + hb
    gm_ref[...] = hf - hb


def _hy_filters(seq, w1, b1, fr1, w2, b2, fr2, w3, decay, name):
    t = jnp.arange(seq, dtype=F32)
    freqs = jnp.linspace(1e-4, HY_BANDS - 1, HY_BANDS, dtype=F32)
    ang = (2.0 * math.pi / seq) * t[:, None] * freqs[None, :]
    feats = jnp.concatenate([(t / (seq - 1))[:, None], jnp.cos(ang), -jnp.sin(ang)], axis=-1)
    hp = LANES
    feats = jnp.pad(feats, ((0, 0), (0, hp - HY_EMB)))
    w1p = jnp.pad(w1, ((0, hp - HY_EMB), (0, hp - HY_FFN)))
    w2p = jnp.pad(w2, ((0, hp - HY_FFN), (0, hp - HY_FFN)))
    w3p = jnp.pad(w3, ((0, hp - HY_FFN), (0, 0))).reshape(hp, 4, W).transpose(1, 0, 2)
    row = lambda v: jnp.pad(v, (0, hp - HY_FFN)).reshape(1, hp)
    dec = decay.reshape(4, 1, W)
    cb = 256
    full = lambda shape: pl.BlockSpec(shape, lambda o, j: (0,) * len(shape))
    gp, gm = pl.pallas_call(
        functools.partial(_hy_filter_body, seq=seq),
        name=name,
        grid=(2, W // cb),
        in_specs=[
            full((seq, hp)), full((hp, hp)), full((1, hp)), full((1, hp)),
            full((hp, hp)), full((1, hp)), full((1, hp)),
            pl.BlockSpec((None, hp, cb), lambda o, j: (2 * o, 0, j)),
            pl.BlockSpec((None, hp, cb), lambda o, j: (2 * o + 1, 0, j)),
            pl.BlockSpec((None, 1, cb), lambda o, j: (2 * o, 0, j)),
            pl.BlockSpec((None, 1, cb), lambda o, j: (2 * o + 1, 0, j)),
        ],
        out_specs=[pl.BlockSpec((seq, cb), lambda o, j: (0, o * (W // cb) + j))] * 2,
        out_shape=[jax.ShapeDtypeStruct((seq, 2 * W), F32)] * 2,
        compiler_params=_cparams("parallel", "parallel"),
    )(feats, w1p, row(b1), row(fr1), w2p, row(b2), row(fr2), w3p, w3p, dec, dec)
    return gp, gm


DFT_RADIX = 64


def _dft_body(ca_ref, sa_ref, cb_ref, sb_ref, ea_ref, eb_ref, c_ref, s_ref):
    ea, eb = ea_ref[...], eb_ref[...]
    ca, sa = _sel_dot(ca_ref[...], ea), _sel_dot(sa_ref[...], ea)
    cb, sb = _sel_dot(cb_ref[...], eb), _sel_dot(sb_ref[...], eb)
    c_ref[...] = (ca * cb - sa * sb).astype(BF16)
    s_ref[...] = (sa * cb + ca * sb).astype(BF16)


def _dft_tables(seq):
    rdx = DFT_RADIX
    r = jnp.arange(seq, dtype=I32)[:, None]
    a = jnp.arange(rdx, dtype=I32)[None, :]
    col = jnp.arange(seq, dtype=I32)[None, :]
    ea = jnp.where(col // rdx == a.T, 1.0, 0.0).astype(BF16)
    eb = jnp.where(col % rdx == a.T, 1.0, 0.0).astype(BF16)

    def small(m):
        ang = (m % (4 * seq)).astype(F32) * (math.pi / (2 * seq))
        return jnp.cos(ang), jnp.sin(ang)

    def build(ma, mb, name):
        (ca, sa), (cb, sb) = small(ma), small(mb)
        tr = _tile(seq, 256)
        sm = pl.BlockSpec((tr, rdx), lambda i: (i, 0))
        ex = pl.BlockSpec((rdx, seq), lambda i: (0, 0))
        return pl.pallas_call(
            _dft_body,
            name=name,
            grid=(seq // tr,),
            in_specs=[sm, sm, sm, sm, ex, ex],
            out_specs=[pl.BlockSpec((tr, seq), lambda i: (i, 0))] * 2,
            out_shape=[jax.ShapeDtypeStruct((seq, seq), BF16)] * 2,
            compiler_params=_cparams("parallel"),
        )(ca, sa, cb, sb, ea, eb)

    c, s = build((2 * r + 1) * (rdx * a), (2 * r + 1) * a, "dft_table")
    ct, st = build(r * (2 * rdx * a), r * (2 * a + 1), "dft_table_t")
    return c, s, ct, st


def _hy_fwd_body(c_ref, s_ref, z_ref, gc_ref, gs_ref, p1_ref, p2_ref):
    z = z_ref[...].astype(BF16)
    zc = _dot(c_ref[...], z)
    zs = _dot(s_ref[...], z)
    gc, gs = gc_ref[...], gs_ref[...]
    p1_ref[...] = (zc * gc - zs * gs).astype(BF16)
    p2_ref[...] = (zc * gs + zs * gc).astype(BF16)


def _hy_fwd(ctab, stab, z, zcol, zrow0, gc, gs, gcol, seq, nseq, name):
    tk = _tile(seq, 512)
    nk = seq // tk
    zb0 = zrow0 // seq
    return pl.pallas_call(
        _hy_fwd_body,
        name=name,
        grid=(nseq, nk),
        in_specs=[
            pl.BlockSpec((tk, seq), lambda b, k: (k, 0)),
            pl.BlockSpec((tk, seq), lambda b, k: (k, 0)),
            pl.BlockSpec((seq, W), lambda b, k: (zb0 + b, zcol)),
            pl.BlockSpec((tk, W), lambda b, k: (k, gcol)),
            pl.BlockSpec((tk, W), lambda b, k: (k, gcol)),
        ],
        out_specs=[pl.BlockSpec((tk, W), lambda b, k: (b * nk + k, 0))] * 2,
        out_shape=[jax.ShapeDtypeStruct((nseq * seq, W), BF16)] * 2,
        compiler_params=_cparams("parallel", "arbitrary"),
    )(ctab, stab, z, gc, gs)


def _hy_inv_body(ct_ref, st_ref, p1_ref, p2_ref, zin_ref, mul_ref, bias_ref, o_ref, *, seq):
    y = (_dot(ct_ref[...], p1_ref[...]) + _dot(st_ref[...], p2_ref[...])) * (1.0 / seq)
    o_ref[...] = mul_ref[...] * (y + bias_ref[...] * zin_ref[...])


def _hy_inv(cttab, sttab, p1, p2, zin, zin_col, zin_row0, mul, mul_col, mul_row0, bias, seq, nseq, name):
    tt = _tile(seq, 512)
    nt = seq // tt
    zr, mr = zin_row0 // tt, mul_row0 // tt
    return pl.pallas_call(
        functools.partial(_hy_inv_body, seq=seq),
        name=name,
        grid=(nseq, nt),
        in_specs=[
            pl.BlockSpec((tt, seq), lambda b, t: (t, 0)),
            pl.BlockSpec((tt, seq), lambda b, t: (t, 0)),
            pl.BlockSpec((seq, W), lambda b, t: (b, 0)),
            pl.BlockSpec((seq, W), lambda b, t: (b, 0)),
            pl.BlockSpec((tt, W), lambda b, t: (zr + b * nt + t, zin_col)),
            pl.BlockSpec((tt, W), lambda b, t: (mr + b * nt + t, mul_col)),
            pl.BlockSpec((1, W), lambda b, t: (0, 0)),
        ],
        out_specs=pl.BlockSpec((tt, W), lambda b, t: (b * nt + t, 0)),
        out_shape=jax.ShapeDtypeStruct((nseq * seq, W), F32),
        compiler_params=_cparams("parallel", "arbitrary"),
    )(cttab, sttab, p1, p2, zin, mul, bias)


def _hyena(u, row0, seq, nseq, tabs, filt, tag):
    ctab, stab, cttab, sttab = tabs
    gc, gs, bias = filt
    p1, p2 = _hy_fwd(ctab, stab, u, 0, row0, gc, gs, 0, seq, nseq, "hy_fwd1" + tag)
    z2 = _hy_inv(cttab, sttab, p1, p2, u, 0, row0, u, 1, row0, bias[0:1], seq, nseq, "hy_inv1" + tag)
    p1, p2 = _hy_fwd(ctab, stab, z2, 0, 0, gc, gs, 1, seq, nseq, "hy_fwd2" + tag)
    return _hy_inv(cttab, sttab, p1, p2, z2, 0, 0, u, 2, row0, bias[1:2], seq, nseq, "hy_inv2" + tag)


def _chunk_index(b, c, rev, n_ctx_chunks, n_lat_chunks, bsz):
    in_ctx = c < n_ctx_chunks
    if rev:
        cc = n_ctx_chunks - 1 - c
        lc = n_lat_chunks - 1 - (c - n_ctx_chunks)
    else:
        cc = c
        lc = c - n_ctx_chunks
    return jnp.where(in_ctx, bsz * n_lat_chunks + b * n_ctx_chunks + cc, b * n_lat_chunks + lc)


def _tri(n, rev):
    r = lax.broadcasted_iota(I32, (n, n), 0)
    c = lax.broadcasted_iota(I32, (n, n), 1)
    return (c >= r) if rev else (c <= r)


def _mlstm_chunk(q, k, v, ig_col, ig_row, b_col, b_row, ct, nrow, m, mask, last):
    a_col = b_col + m
    dmat = jnp.where(mask, b_col - b_row + ig_row, -jnp.inf)
    mt = jnp.maximum(a_col, jnp.max(dmat, axis=1, keepdims=True))
    inter = jnp.exp(a_col - mt)
    qb, kb = q.astype(BF16), k.astype(BF16)
    s = _dot_nt(qb, kb) * jnp.exp(dmat - mt)
    num = _dot(s.astype(BF16), v.astype(BF16)) + inter * _dot(qb, ct.astype(BF16))
    den = jnp.sum(s, axis=1, keepdims=True) + inter * jnp.sum(q * nrow, axis=1, keepdims=True)
    h = num / jnp.maximum(jnp.abs(den), jnp.exp(-mt))
    m_new = mt[last:last + 1, :]
    tot = b_col[last:last + 1, :]
    ws = jnp.exp(tot - b_col + ig_col - m_new)
    dec = jnp.exp(tot + m - m_new)
    ct_new = dec * ct + _dot(k.T.astype(BF16), (v * ws).astype(BF16))
    n_new = dec * nrow + jnp.sum(k * ws, axis=0, keepdims=True)
    return h, ct_new, n_new, m_new


def _mlstm_body(qkf_ref, vf_ref, gcf_ref, grf_ref, qkb_ref, vb_ref, gcb_ref, grb_ref, gbc_ref, gbr_ref,
                hf_ref, hb_ref, ct_scr, n_scr, m_scr):
    T = ML_CHUNK

    @pl.when(pl.program_id(1) == 0)
    def _():
        ct_scr[...] = jnp.zeros_like(ct_scr)
        n_scr[...] = jnp.zeros_like(n_scr)
        m_scr[...] = jnp.zeros_like(m_scr)

    dirs = ((qkf_ref, vf_ref, gcf_ref, grf_ref, hf_ref), (qkb_ref, vb_ref, gcb_ref, grb_ref, hb_ref))
    for d, (qk_ref, v_ref, gc_ref, gr_ref, h_ref) in enumerate(dirs):
        rev = d == 1
        mask = _tri(T, rev)
        m01 = jnp.where(mask, 1.0, 0.0).astype(BF16)
        m01t = jnp.where(_tri(T, not rev), 1.0, 0.0).astype(BF16)
        gcol = gc_ref[:, 0:16] + gbc_ref[...]
        grow = gr_ref[0:16, :] + gbr_ref[...]
        ls_col, ls_row = _log_sigmoid(gcol), _log_sigmoid(grow)
        last = 0 if rev else T - 1
        nsub = SCAN_BLOCK // T
        carry = [(ct_scr[d * ML_HEADS + h], n_scr[d * ML_HEADS + h:d * ML_HEADS + h + 1, :],
                  m_scr[d * ML_HEADS + h:d * ML_HEADS + h + 1, 0:1]) for h in range(ML_HEADS)]
        for sub in (range(nsub - 1, -1, -1) if rev else range(nsub)):
            rs = slice(sub * T, (sub + 1) * T)
            cum_col = _dot_sel(m01, ls_col[rs])
            cum_row = _sel_dot(ls_row[:, rs], m01t)
            for h in range(ML_HEADS):
                ci, cf = 2 * d * ML_HEADS + h, (2 * d + 1) * ML_HEADS + h
                hs = slice(h * ML_HD, (h + 1) * ML_HD)
                ks = slice(W + h * ML_HD, W + (h + 1) * ML_HD)
                ct, nrow, m = carry[h]
                hh, ct, nrow, m = _mlstm_chunk(
                    qk_ref[rs, hs], qk_ref[rs, ks], v_ref[rs, hs],
                    gcol[rs, ci:ci + 1], grow[ci:ci + 1, rs], cum_col[:, cf:cf + 1], cum_row[cf:cf + 1, :],
                    ct, nrow, m, mask, last)
                h_ref[rs, hs] = hh
                carry[h] = (ct, nrow, m)
        for h in range(ML_HEADS):
            ct, nrow, m = carry[h]
            j = d * ML_HEADS + h
            ct_scr[j] = ct
            n_scr[j:j + 1, :] = nrow
            m_scr[j:j + 1, :] = jnp.broadcast_to(m, (1, LANES))


def _mlstm_scan(qk, p, small_t, gate_b, bsz, seq, ctx_len):
    T = SCAN_BLOCK
    rows = qk.shape[0]
    ncc, nlc = ctx_len // T, seq // T
    cf = functools.partial(_chunk_index, rev=False, n_ctx_chunks=ncc, n_lat_chunks=nlc, bsz=bsz)
    cr = functools.partial(_chunk_index, rev=True, n_ctx_chunks=ncc, n_lat_chunks=nlc, bsz=bsz)

    def specs(ci):
        return [pl.BlockSpec((T, 2 * W), lambda b, c: (ci(b, c), 0)),
                pl.BlockSpec((T, W), lambda b, c: (ci(b, c), P_MLV // W)),
                pl.BlockSpec((T, LANES), lambda b, c: (ci(b, c), P_SMALL // LANES)),
                pl.BlockSpec((N_SMALL, T), lambda b, c: (0, ci(b, c)))]

    nchain = 2 * ML_HEADS
    return pl.pallas_call(
        _mlstm_body,
        name="mlstm_scan",
        grid=(bsz, ncc + nlc),
        in_specs=specs(cf) + specs(cr) + [pl.BlockSpec((1, 16), lambda b, c: (0, 0)),
                                         pl.BlockSpec((16, 1), lambda b, c: (0, 0))],
        out_specs=[pl.BlockSpec((T, W), lambda b, c: (cf(b, c), 0)), pl.BlockSpec((T, W), lambda b, c: (cr(b, c), 0))],
        out_shape=[jax.ShapeDtypeStruct((rows, W), F32)] * 2,
        scratch_shapes=[pltpu.VMEM((nchain, ML_HD, ML_HD), F32), pltpu.VMEM((nchain, LANES), F32),
                        pltpu.VMEM((nchain, LANES), F32)],
        compiler_params=_cparams("parallel", "arbitrary"),
    )(qk, p, p, small_t, qk, p, p, small_t, gate_b.reshape(1, 16), gate_b.reshape(16, 1))


def _mlstm_fin_body(hf_ref, hb_ref, o_ref, g_ref, y_ref):
    h = hf_ref[...] + hb_ref[...]
    for i in range(ML_HEADS):
        hh = h[:, i * ML_HD:(i + 1) * ML_HD]
        hh = hh * lax.rsqrt(jnp.mean(hh * hh, axis=-1, keepdims=True) + EPS)
        sl = slice(i * ML_HD, (i + 1) * ML_HD)
        y_ref[:, sl] = hh * g_ref[:, sl] * _sigmoid(o_ref[:, sl])


def _mlstm_finish(hf, hb, p, norm_g, rows):
    tm = _tile(rows, 512)
    return pl.pallas_call(
        _mlstm_fin_body,
        name="mlstm_finish",
        grid=(rows // tm,),
        in_specs=[pl.BlockSpec((tm, W), lambda i: (i, 0)), pl.BlockSpec((tm, W), lambda i: (i, 0)),
                  pl.BlockSpec((tm, W), lambda i: (i, P_MLO // W)), pl.BlockSpec((1, W), lambda i: (0, 0))],
        out_specs=pl.BlockSpec((tm, W), lambda i: (i, 0)),
        out_shape=jax.ShapeDtypeStruct((rows, W), F32),
        compiler_params=_cparams("parallel"),
    )(hf, hb, p, norm_g)


def _ssd_body(xf_ref, dcf_ref, drf_ref, xb_ref, dcb_ref, drb_ref, dbc_ref, dbr_ref, ac_ref, ar_ref,
              yf_ref, yb_ref, st_scr):
    T = SSD_CHUNK

    @pl.when(pl.program_id(1) == 0)
    def _():
        st_scr[...] = jnp.zeros_like(st_scr)

    for d, (xbc_ref, dcol_ref, drow_ref, y_ref) in enumerate(((xf_ref, dcf_ref, drf_ref, yf_ref),
                                                              (xb_ref, dcb_ref, drb_ref, yb_ref))):
        rev = d == 1
        mask = _tri(T, rev)
        m01 = jnp.where(mask, 1.0, 0.0).astype(BF16)
        m01t = jnp.where(_tri(T, not rev), 1.0, 0.0).astype(BF16)
        dt_col = _softplus(dcol_ref[:, 16:32] + dbc_ref[...])
        dt_row = _softplus(drow_ref[16:32, :] + dbr_ref[...])
        acs_col = _dot_sel(m01, dt_col * ac_ref[...])
        acs_row = _sel_dot(dt_row * ar_ref[...], m01t)
        last = 0 if rev else T - 1
        for g in range(SSD_GROUPS):
            bm = xbc_ref[:, W + g * SSD_STATE:W + (g + 1) * SSD_STATE]
            cm = xbc_ref[:, W + (SSD_GROUPS + g) * SSD_STATE:W + (SSD_GROUPS + g + 1) * SSD_STATE]
            bmb, cmb = bm.astype(BF16), cm.astype(BF16)
            cb = _dot_nt(cmb, bmb)
            bmt = bm.T.astype(BF16)
            for hh in range(SSD_HEADS // SSD_GROUPS):
                h = g * (SSD_HEADS // SSD_GROUPS) + hh
                ci = d * SSD_HEADS + h
                a_col, a_row = acs_col[:, ci:ci + 1], acs_row[ci:ci + 1, :]
                x = xbc_ref[:, h * SSD_HD:(h + 1) * SSD_HD] * dt_col[:, ci:ci + 1]
                lm = jnp.where(mask, jnp.exp(jnp.where(mask, a_col - a_row, 0.0)), 0.0)
                st = st_scr[ci]
                y = _dot((cb * lm).astype(BF16), x.astype(BF16)) + _dot(cmb, st.astype(BF16)) * jnp.exp(a_col)
                y_ref[:, h * SSD_HD:(h + 1) * SSD_HD] = y
                tot = a_col[last:last + 1, :]
                xd = x * jnp.exp(tot - a_col)
                st_scr[ci] = jnp.exp(tot) * st + _dot(bmt, xd.astype(BF16))


def _ssd_scan(xbc, p, small_t, dt_bias, a_log, bsz, seq, ctx_len):
    T = SSD_CHUNK
    rows = xbc.shape[0]
    ncc, nlc = ctx_len // T, seq // T
    cf = functools.partial(_chunk_index, rev=False, n_ctx_chunks=ncc, n_lat_chunks=nlc, bsz=bsz)
    cr = functools.partial(_chunk_index, rev=True, n_ctx_chunks=ncc, n_lat_chunks=nlc, bsz=bsz)
    a = -jnp.exp(a_log.astype(F32))
    small = lambda shape: pl.BlockSpec(shape, lambda b, c: (0, 0))

    def specs(ci):
        return [pl.BlockSpec((T, 2 * W), lambda b, c: (ci(b, c), 0)),
                pl.BlockSpec((T, LANES), lambda b, c: (ci(b, c), P_SMALL // LANES)),
                pl.BlockSpec((N_SMALL, T), lambda b, c: (0, ci(b, c)))]

    return pl.pallas_call(
        _ssd_body,
        name="ssd_scan",
        grid=(bsz, ncc + nlc),
        in_specs=specs(cf) + specs(cr) + [small((1, 16)), small((16, 1)), small((1, 16)), small((16, 1))],
        out_specs=[pl.BlockSpec((T, W), lambda b, c: (cf(b, c), 0)), pl.BlockSpec((T, W), lambda b, c: (cr(b, c), 0))],
        out_shape=[jax.ShapeDtypeStruct((rows, W), F32)] * 2,
        scratch_shapes=[pltpu.VMEM((2 * SSD_HEADS, SSD_STATE, SSD_HD), F32)],
        compiler_params=_cparams("parallel", "arbitrary"),
    )(xbc, p, small_t, xbc, p, small_t, dt_bias.reshape(1, 16), dt_bias.reshape(16, 1), a.reshape(1, 16),
      a.reshape(16, 1))


def _ssd_fin_body(yf_ref, yb_ref, x_ref, z_ref, dsk_ref, g_ref, o_ref):
    y = yf_ref[...] + yb_ref[...] + dsk_ref[...] * x_ref[...]
    y = y * _silu(z_ref[...])
    o_ref[...] = y * lax.rsqrt(jnp.mean(y * y, axis=-1, keepdims=True) + EPS) * g_ref[...]


def _ssd_finish(yf, yb, xbc, p, d_skip, norm_g, rows):
    tm = _tile(rows, 512)
    blk = lambda col: pl.BlockSpec((tm, W), lambda i: (i, col))
    vec = pl.BlockSpec((1, W), lambda i: (0, 0))
    return pl.pallas_call(
        _ssd_fin_body,
        name="ssd_finish",
        grid=(rows // tm,),
        in_specs=[blk(0), blk(0), blk(0), blk(P_SSDZ // W), vec, vec],
        out_specs=blk(0),
        out_shape=jax.ShapeDtypeStruct((rows, W), F32),
        compiler_params=_cparams("parallel"),
    )(yf, yb, xbc, p, jnp.repeat(d_skip, SSD_HD).reshape(1, W), norm_g)


def _s5_disc_body(are_ref, aim_ref, ldt_ref, bre_ref, bim_ref, abre_ref, abim_ref, bbre_ref, bbim_ref):
    lam_re = jnp.minimum(are_ref[...], -1e-4)
    a_im = aim_ref[...]
    dt = jnp.exp(ldt_ref[...])
    mag = jnp.exp(lam_re * dt)
    ab_re, ab_im = mag * jnp.cos(a_im * dt), mag * jnp.sin(a_im * dt)
    den = lam_re * lam_re + a_im * a_im
    nr, ni = ab_re - 1.0, ab_im
    f_re = (nr * lam_re + ni * a_im) / den
    f_im = (ni * lam_re - nr * a_im) / den
    abre_ref[...] = ab_re
    abim_ref[...] = ab_im
    bbre_ref[...] = f_re * bre_ref[...] - f_im * bim_ref[...]
    bbim_ref[...] = f_re * bim_ref[...] + f_im * bre_ref[...]


def _s5_discretise(a_re, a_im, log_dt, b_re, b_im):
    gn = S5_NS
    col = lambda v: v.reshape(gn, 1)
    ldt = jnp.repeat(log_dt, S5_STATE).reshape(gn, 1)
    mat = lambda v: v.reshape(gn, S5_GROUP)
    cs = pl.BlockSpec((gn, 1), lambda: (0, 0))
    ms = pl.BlockSpec((gn, S5_GROUP), lambda: (0, 0))
    return pl.pallas_call(
        _s5_disc_body,
        name="s5_discretise",
        in_specs=[cs, cs, cs, ms, ms],
        out_specs=[cs, cs, ms, ms],
        out_shape=[jax.ShapeDtypeStruct((gn, 1), F32)] * 2 + [jax.ShapeDtypeStruct((gn, S5_GROUP), F32)] * 2,
    )(col(a_re), col(a_im), ldt, mat(b_re), mat(b_im))


def _s5_body(*refs, nchain, bsz):
    u_refs = refs[:nchain]
    pm_ref, pmt_ref, wb_ref, a_ref, wc_ref, y_ref, x_scr, st_scr = refs[nchain:]
    T = S5_T
    sw = S5_NS // S5_SUPER
    cw = W // S5_SUPER

    @pl.when(pl.program_id(0) == 0)
    def _():
        st_scr[...] = jnp.zeros_like(st_scr)

    stack = jnp.concatenate([r[...] for r in u_refs], axis=0).astype(BF16)
    lhs = _dot(pm_ref[...], stack)
    chain = lax.broadcasted_iota(I32, lhs.shape, 0) % nchain
    lhs_f = jnp.where(chain < bsz, lhs, 0.0).astype(BF16)
    lhs_b = jnp.where(chain < bsz, 0.0, lhs).astype(BF16)
    ys = []
    for g in range(S5_SUPER):
        cs = slice(g * cw, (g + 1) * cw)
        re = slice(g * sw, (g + 1) * sw)
        im = slice(S5_NS + g * sw, S5_NS + (g + 1) * sw)
        bu = _dot(jnp.concatenate([lhs_f[:, cs], lhs_b[:, cs]], axis=1), wb_ref[g])
        x_scr[:, re] = bu[:, :sw]
        x_scr[:, im] = bu[:, sw:]
        ar, ai = a_ref[:, re], a_ref[:, im]

        def step(j, carry):
            sr, si = carry
            r0 = pl.multiple_of(j * nchain, nchain)
            nr = ar * sr - ai * si + x_scr[pl.ds(r0, nchain), re]
            ni = ar * si + ai * sr + x_scr[pl.ds(r0, nchain), im]
            x_scr[pl.ds(r0, nchain), re] = nr
            x_scr[pl.ds(r0, nchain), im] = ni
            return nr, ni

        sr, si = lax.fori_loop(0, T, step, (st_scr[:, re], st_scr[:, im]))
        st_scr[:, re] = sr
        st_scr[:, im] = si
        wc = wc_ref[g]
        ys.append(_dot(x_scr[:, re].astype(BF16), wc[:sw]) + _dot(x_scr[:, im].astype(BF16), wc[sw:]))
    y = _dot_sel(pmt_ref[...], jnp.concatenate(ys, axis=1))
    for c in range(nchain):
        y_ref[c // bsz, c % bsz] = y[c * T:(c + 1) * T]


def _s5_scan(p, pm, pmt, wb, atab, wc, bsz, seq, ctx_len):
    T = S5_T
    nchain = 2 * bsz
    ncc, nlc = ctx_len // T, seq // T
    nsteps = ncc + nlc
    rows = nchain * T
    const = lambda shape: pl.BlockSpec(shape, lambda i: (0,) * len(shape))
    u_specs = []
    for c in range(nchain):
        rev = c >= bsz
        u_specs.append(pl.BlockSpec(
            (T, W), functools.partial(lambda i, b, rev: (_chunk_index(b, i, rev, ncc, nlc, bsz), P_S5 // W),
                                      b=c % bsz, rev=rev)))
    return pl.pallas_call(
        functools.partial(_s5_body, nchain=nchain, bsz=bsz),
        name="s5_scan",
        grid=(nsteps,),
        in_specs=u_specs + [const((rows, rows)), const((rows, rows)), const((S5_SUPER, 2 * W // S5_SUPER, 2 * S5_NS // S5_SUPER)),
                            const((nchain, 2 * S5_NS)), const((S5_SUPER, 2 * S5_NS // S5_SUPER, W // S5_SUPER))],
        out_specs=pl.BlockSpec((2, bsz, None, T, W), lambda i: (0, 0, i, 0, 0)),
        out_shape=jax.ShapeDtypeStruct((2, bsz, nsteps, T, W), F32),
        scratch_shapes=[pltpu.VMEM((rows, 2 * S5_NS), F32), pltpu.VMEM((nchain, 2 * S5_NS), F32)],
        compiler_params=_cparams("arbitrary"),
    )(*([p] * nchain), pm, pmt, wb, atab, wc)


def _s5_fin_body(yf_ref, yb_ref, u_ref, d_ref, w_ref, b_ref, o_ref, *, nchunk):
    yf = yf_ref[...].reshape(nchunk * S5_T, W)
    yb = jnp.concatenate([yb_ref[nchunk - 1 - q] for q in range(nchunk)], axis=0)
    y = yf + yb + d_ref[...] * u_ref[...]
    g = _gelu_tanh(y)
    o_ref[...] = g * _sigmoid(_dot(g.astype(BF16), w_ref[...].astype(BF16)) + b_ref[...])


def _s5_finish(ys, p, d_skip, glu_w, glu_b, rows, bsz, seq, ctx_len):
    tile = ctx_len
    g = tile // S5_T
    nlt = seq // tile
    n_lat = bsz * nlt
    samp = lambda i: jnp.where(i < n_lat, i // nlt, i - n_lat)
    fblk = lambda i: jnp.where(i < n_lat, 1 + i % nlt, 0)
    bblk = lambda i: jnp.where(i < n_lat, nlt - i % nlt, 0)
    vec = pl.BlockSpec((1, W), lambda i: (0, 0))
    return pl.pallas_call(
        functools.partial(_s5_fin_body, nchunk=g),
        name="s5_finish",
        grid=(rows // tile,),
        in_specs=[pl.BlockSpec((None, None, g, S5_T, W), lambda i: (0, samp(i), fblk(i), 0, 0)),
                  pl.BlockSpec((None, None, g, S5_T, W), lambda i: (1, samp(i), bblk(i), 0, 0)),
                  pl.BlockSpec((tile, W), lambda i: (i, P_S5 // W)), vec,
                  pl.BlockSpec((W, W), lambda i: (0, 0)), vec],
        out_specs=pl.BlockSpec((tile, W), lambda i: (i, 0)),
        out_shape=jax.ShapeDtypeStruct((rows, W), F32),
        compiler_params=_cparams("parallel"),
    )(ys, ys, p, d_skip.reshape(1, W), glu_w, glu_b.reshape(1, W))


def _block_diag(m):
    g, a, b = m.shape
    eye = jnp.eye(g, dtype=m.dtype)
    return (eye[:, None, :, None] * m[:, :, None, :]).reshape(g * a, g * b)


def _s5_branch(p, rows, bsz, seq, ctx_len, a_re, a_im, log_dt, b_re, b_im, c_re, c_im, d_skip, glu_w, glu_b):
    disc = [_s5_discretise(a_re[d], a_im[d], log_dt[d], b_re, b_im) for d in range(2)]
    gs = S5_GROUPS // S5_SUPER
    def bmat(v):
        v = v.reshape(S5_SUPER, gs, S5_STATE, S5_GROUP).transpose(0, 1, 3, 2)
        return jnp.stack([_block_diag(v[s]) for s in range(S5_SUPER)])
    wb = jnp.concatenate([jnp.concatenate([bmat(disc[d][2]), bmat(disc[d][3])], axis=2) for d in range(2)],
                         axis=1).astype(BF16)
    arow = lambda d: jnp.concatenate([disc[d][0].reshape(1, S5_NS), disc[d][1].reshape(1, S5_NS)], axis=1)
    atab = jnp.concatenate([jnp.broadcast_to(arow(0), (bsz, 2 * S5_NS)),
                            jnp.broadcast_to(arow(1), (bsz, 2 * S5_NS))], axis=0)
    def cmat(v):
        v = v.reshape(S5_SUPER, gs, S5_GROUP, S5_STATE).transpose(0, 1, 3, 2)
        return jnp.stack([_block_diag(v[s]) for s in range(S5_SUPER)])
    wc = jnp.concatenate([cmat(c_re), -cmat(c_im)], axis=1).astype(BF16)
    T, nchain = S5_T, 2 * bsz
    r = jnp.arange(nchain * T, dtype=I32)
    j, c = r // nchain, r % nchain
    src = c * T + jnp.where(c < bsz, j, T - 1 - j)
    pm = (src[:, None] == r[None, :])
    ys = _s5_scan(p, pm.astype(BF16), pm.T.astype(BF16), wb, atab, wc, bsz, seq, ctx_len)
    return _s5_finish(ys, p, d_skip, glu_w, glu_b, rows, bsz, seq, ctx_len)


def _merge_body(y0l, y0c, y1, y2, y3, g0, g1, g2, g3, w0, w1, w2, w3, o_ref, *, n_lat_tiles):
    hy = jnp.where(pl.program_id(0) < n_lat_tiles, y0l[...], y0c[...])
    acc = _sigmoid(g0[...]) * _dot(hy.astype(BF16), w0[...])
    for y, g, w in ((y1, g1, w1), (y2, g2, w2), (y3, g3, w3)):
        acc = acc + _sigmoid(g[...]) * _dot(y[...].astype(BF16), w[...])
    o_ref[...] = acc.astype(BF16)


def _merge(hy_lat, hy_ctx, ys, p, wb, rows):
    nl = hy_lat.shape[0]
    tm, tn = _tile(math.gcd(nl, hy_ctx.shape[0]), 512), 512
    n_lat = nl // tm
    ysp = pl.BlockSpec((tm, W), lambda i, j: (i, 0))
    gsp = lambda k: pl.BlockSpec((tm, tn), lambda i, j: (i, (P_GATE + k * D_MODEL) // tn + j))
    wsp = lambda k: pl.BlockSpec((None, W, tn), lambda i, j: (k, 0, j))
    return pl.pallas_call(
        functools.partial(_merge_body, n_lat_tiles=n_lat),
        name="merge",
        grid=(rows // tm, D_MODEL // tn),
        in_specs=[pl.BlockSpec((tm, W), lambda i, j: (jnp.minimum(i, n_lat - 1), 0)),
                  pl.BlockSpec((tm, W), lambda i, j: (jnp.maximum(i - n_lat, 0), 0))]
                 + [ysp] * 3 + [gsp(k) for k in range(4)] + [wsp(k) for k in range(4)],
        out_specs=pl.BlockSpec((tm, tn), lambda i, j: (i, j)),
        out_shape=jax.ShapeDtypeStruct((rows, D_MODEL), BF16),
        compiler_params=_cparams("parallel", "parallel"),
    )(hy_lat, hy_ctx, *ys, p, p, p, p, wb, wb, wb, wb)


def _outproj_body(a_ref, w_ref, xl_ref, xc_ref, mod_ref, o_ref, *, n_lat_tiles):
    y = mod_ref[2:3, :] * _dot(a_ref[...], w_ref[...])
    i = pl.program_id(0)

    @pl.when(i < n_lat_tiles)
    def _():
        o_ref[...] = xl_ref[...] + y

    @pl.when(i >= n_lat_tiles)
    def _():
        o_ref[...] = xc_ref[...] + y


def _outproj(acc, w, xl, xc, mods, seq, bsz):
    rows = acc.shape[0]
    nl, nc = xl.shape[0], xc.shape[0]
    tm = _tile(math.gcd(nl, nc), 512)
    tn = 512
    n_lat = nl // tm
    return pl.pallas_call(
        functools.partial(_outproj_body, n_lat_tiles=n_lat),
        name="outproj",
        grid=(rows // tm, D_MODEL // tn),
        in_specs=[
            pl.BlockSpec((tm, D_MODEL), lambda i, j: (i, 0)),
            pl.BlockSpec((D_MODEL, tn), lambda i, j: (0, j)),
            pl.BlockSpec((tm, tn), lambda i, j: (jnp.minimum(i, n_lat - 1), j)),
            pl.BlockSpec((tm, tn), lambda i, j: (jnp.maximum(i - n_lat, 0), j)),
            pl.BlockSpec((None, 6, tn), lambda i, j: (_mod_index(i, tm, n_lat, seq, bsz), 0, j)),
        ],
        out_specs=pl.BlockSpec((tm, tn), lambda i, j: (i, j)),
        out_shape=jax.ShapeDtypeStruct((rows, D_MODEL), F32),
        compiler_params=_cparams("parallel", "parallel"),
    )(acc, w, xl, xc, mods)


def _router_body(x_ref, g_ref, mod_ref, wr_ref, h_ref, p_ref):
    h = _norm_mod(x_ref[...], g_ref[...], mod_ref[3:4, :], mod_ref[4:5, :])
    h_ref[...] = h.astype(BF16)
    logits = _dot3(h, wr_ref[...])
    lane = lax.broadcasted_iota(I32, logits.shape, 1)
    logits = jnp.where(lane < N_EXPERTS, logits, -jnp.inf)
    e = jnp.exp(logits - jnp.max(logits, axis=-1, keepdims=True))
    p_ref[...] = e / jnp.sum(e, axis=-1, keepdims=True)


def _router(xm, g, mods, w_router, n_lat_rows, seq, bsz):
    rows = xm.shape[0]
    tm = _tile(math.gcd(n_lat_rows, seq), 512)
    n_lat = n_lat_rows // tm
    wr = jnp.pad(w_router, ((0, 0), (0, LANES - N_EXPERTS)))
    return pl.pallas_call(
        _router_body,
        name="moe_router",
        grid=(rows // tm,),
        in_specs=[
            pl.BlockSpec((tm, D_MODEL), lambda i: (i, 0)),
            pl.BlockSpec((1, D_MODEL), lambda i: (0, 0)),
            pl.BlockSpec((None, 6, D_MODEL), lambda i: (_mod_index(i, tm, n_lat, seq, bsz), 0, 0)),
            pl.BlockSpec((D_MODEL, LANES), lambda i: (0, 0)),
        ],
        out_specs=[pl.BlockSpec((tm, D_MODEL), lambda i: (i, 0)), pl.BlockSpec((tm, LANES), lambda i: (i, 0))],
        out_shape=[jax.ShapeDtypeStruct((rows, D_MODEL), BF16), jax.ShapeDtypeStruct((rows, LANES), F32)],
        compiler_params=_cparams("parallel"),
    )(xm, g, mods, wr)


def _route_geometry(n):
    cap = EC_CAPACITY * n // N_EXPERTS
    blk = min(n, ROUTE_BLOCK)
    nblk = n // blk
    nb1 = -(-(nblk + 1) // 8) * 8
    st = min(cap, LANES)
    return cap, blk, nblk, nb1, st


def _topk_body(p_ref, o_ref, cnt_ref, *, n, cap, blk):
    bits = pltpu.bitcast(p_ref[...], I32)
    capf = float(cap)

    def count(mask):
        return jnp.sum(jnp.where(mask, 1.0, 0.0), axis=0, keepdims=True)

    def vstep(i, thr):
        cand = thr | lax.shift_left(jnp.int32(1), 30 - i)
        return jnp.where(count(bits >= cand) >= capf, cand, thr)

    thr = lax.fori_loop(0, 31, vstep, jnp.zeros((1, LANES), I32))
    gt = bits > thr
    eq = bits == thr
    need = capf - count(gt)
    t = lax.broadcasted_iota(I32, bits.shape, 0)
    nbits = max(1, (n - 1).bit_length())

    def istep(i, j):
        cand = j + lax.shift_left(jnp.int32(1), nbits - 1 - i)
        return jnp.where(count(eq & (t < cand)) < need, cand, j)

    jmax = lax.fori_loop(0, nbits, istep, jnp.zeros((1, LANES), I32))
    sel = gt | (eq & (t <= jmax))
    self32 = jnp.where(sel, 1.0, 0.0)
    r = lax.broadcasted_iota(I32, (blk, blk), 0)
    c = lax.broadcasted_iota(I32, (blk, blk), 1)
    lower = jnp.where(c < r, 1.0, 0.0).astype(BF16)
    carry = jnp.zeros((1, LANES), F32)
    cnt_ref[...] = jnp.zeros_like(cnt_ref)
    for i in range(n // blk):
        sb = self32[i * blk:(i + 1) * blk]
        rank = _dot(lower, sb.astype(BF16)) + carry
        o_ref[i * blk:(i + 1) * blk, :] = jnp.where(sb > 0.0, rank, -1.0).astype(I32)
        carry = carry + jnp.sum(sb, axis=0, keepdims=True)
        cnt_ref[i + 1:i + 2, :] = carry.astype(I32)


def _topk(probs, row0, n, nsets, name):
    cap, blk, nblk, nb1, _ = _route_geometry(n)
    return pl.pallas_call(
        functools.partial(_topk_body, n=n, cap=cap, blk=blk),
        name=name,
        grid=(nsets,),
        in_specs=[pl.BlockSpec((n, LANES), lambda s: (row0 // n + s, 0))],
        out_specs=[pl.BlockSpec((n, LANES), lambda s: (s, 0)), pl.BlockSpec((None, nb1, LANES), lambda s: (s, 0, 0))],
        out_shape=[jax.ShapeDtypeStruct((nsets * n, LANES), I32), jax.ShapeDtypeStruct((nsets, nb1, LANES), I32)],
        compiler_params=_cparams("parallel"),
    )(probs)


def _gather_body(cnt_ref, slot_ref, p_ref, h_ref, xs_ref, gate_ref, acc, gacc, *, cap, blk, nblk, nb1, st):
    s = pl.program_id(0)
    e = pl.program_id(2)
    bounds = [cnt_ref[(s * nb1 + j) * N_EXPERTS + e] for j in range(nblk + 1)]
    lane = lax.broadcasted_iota(I32, (st, LANES), 1)
    for q in range(cap // st):
        lo, hi = q * st, (q + 1) * st
        b_lo = sum((bounds[j + 1] <= lo).astype(I32) for j in range(nblk))
        b_hi = sum((bounds[j] < hi).astype(I32) for j in range(nblk))
        rr = lax.broadcasted_iota(I32, (st, blk), 0) + lo
        acc[...] = jnp.zeros_like(acc)
        gacc[...] = jnp.zeros_like(gacc)

        def body(j, _):
            r0 = pl.multiple_of(j * blk, blk)
            onehot = jnp.where(slot_ref[e, pl.ds(j, 1), :] == rr, 1.0, 0.0).astype(BF16)
            acc[...] += _dot(onehot, h_ref[pl.ds(r0, blk), :])
            gacc[...] += _dot_sel(onehot, p_ref[pl.ds(r0, blk), :])
            return 0

        lax.fori_loop(b_lo, b_hi, body, 0)
        xs_ref[lo:hi, :] = acc[...].astype(BF16)
        gate_ref[lo:hi, :] = jnp.sum(jnp.where(lane == e, gacc[...], 0.0), axis=1, keepdims=True)


def _gather(slot_col, counts, probs, h2, row0, n, nsets, name):
    cap, blk, nblk, nb1, st = _route_geometry(n)
    dh = D_MODEL // 2
    slot_row = slot_col.reshape(nsets, nblk, blk, LANES)[..., :N_EXPERTS].transpose(0, 3, 1, 2)
    cnt = counts[:, :, :N_EXPERTS].reshape(-1)
    grid_spec = pltpu.PrefetchScalarGridSpec(
        num_scalar_prefetch=1,
        grid=(nsets, 2, N_EXPERTS),
        in_specs=[
            pl.BlockSpec((None, N_EXPERTS, nblk, blk), lambda s, k, e, c: (s, 0, 0, 0)),
            pl.BlockSpec((n, LANES), lambda s, k, e, c: (row0 // n + s, 0)),
            pl.BlockSpec((n, dh), lambda s, k, e, c: (row0 // n + s, k)),
        ],
        out_specs=[pl.BlockSpec((None, cap, dh), lambda s, k, e, c: (e, s, k)),
                   pl.BlockSpec((None, None, cap, 1), lambda s, k, e, c: (k, e, s, 0))],
        scratch_shapes=[pltpu.VMEM((st, dh), F32), pltpu.VMEM((st, LANES), F32)],
    )
    xs, gate = pl.pallas_call(
        functools.partial(_gather_body, cap=cap, blk=blk, nblk=nblk, nb1=nb1, st=st),
        name=name,
        grid_spec=grid_spec,
        out_shape=[jax.ShapeDtypeStruct((N_EXPERTS, nsets * cap, D_MODEL), BF16),
                   jax.ShapeDtypeStruct((2, N_EXPERTS, nsets * cap, 1), F32)],
        compiler_params=_cparams("parallel", "arbitrary", "arbitrary"),
    )(cnt, slot_row, probs, h2)
    return xs, gate[0]


def _ffn_body(*refs, with_ctx, ml):
    if with_ctx:
        xl_ref, gl_ref, xc_ref, gc_ref, wg_ref, wu_ref, wd_ref, yl_ref, yc_ref, acc = refs
        xs = jnp.concatenate([xl_ref[...], xc_ref[...]], axis=0)
    else:
        xl_ref, gl_ref, wg_ref, wu_ref, wd_ref, yl_ref, acc = refs
        xs = xl_ref[...]
    f = pl.program_id(2)

    @pl.when(f == 0)
    def _():
        acc[...] = jnp.zeros_like(acc)

    hid = _silu(_dot(xs, wg_ref[...].astype(BF16))) * _dot(xs, wu_ref[...].astype(BF16))
    acc[...] += _dot(hid.astype(BF16), wd_ref[...].astype(BF16))

    @pl.when(f == pl.num_programs(2) - 1)
    def _():
        yl_ref[...] = (acc[0:ml, :] * gl_ref[...]).astype(BF16)
        if with_ctx:
            yc_ref[...] = (acc[ml:, :] * gc_ref[...]).astype(BF16)


def _ffn(xs_l, g_l, xs_c, g_c, w_gate, w_up, w_down):
    with_ctx = xs_c is not None
    nsplit = 2
    ml = xs_l.shape[1] // nsplit
    mc = xs_c.shape[1] // nsplit if with_ctx else 0
    fc = 256
    row = lambda m, last: pl.BlockSpec((None, m, last), lambda e, s, f: (e, s, 0))
    in_specs = [row(ml, D_MODEL), row(ml, 1)]
    args = [xs_l, g_l]
    out_specs = [row(ml, D_MODEL)]
    out_shape = [jax.ShapeDtypeStruct(xs_l.shape, BF16)]
    if with_ctx:
        in_specs += [row(mc, D_MODEL), row(mc, 1)]
        args += [xs_c, g_c]
        out_specs.append(row(mc, D_MODEL))
        out_shape.append(jax.ShapeDtypeStruct(xs_c.shape, BF16))
    in_specs += [pl.BlockSpec((None, D_MODEL, fc), lambda e, s, f: (e, 0, f)),
                 pl.BlockSpec((None, D_MODEL, fc), lambda e, s, f: (e, 0, f)),
                 pl.BlockSpec((None, fc, D_MODEL), lambda e, s, f: (e, f, 0))]
    args += [w_gate, w_up, w_down]
    out = pl.pallas_call(
        functools.partial(_ffn_body, with_ctx=with_ctx, ml=ml),
        name="moe_ffn",
        grid=(N_EXPERTS, nsplit, D_EXPERT // fc),
        in_specs=in_specs,
        out_specs=out_specs,
        out_shape=out_shape,
        scratch_shapes=[pltpu.VMEM((ml + mc, D_MODEL), F32)],
        compiler_params=_cparams("parallel", "parallel", "arbitrary"),
    )(*args)
    return (out[0], out[1]) if with_ctx else (out[0], None)


def _scatter_body(cnt_ref, slot_ref, y_ref, x_ref, mod_ref, o_ref, acc, *, st, nb1, bpt):
    s, t, e = pl.program_id(0), pl.program_id(1), pl.program_id(2)

    @pl.when(e == 0)
    def _():
        acc[...] = jnp.zeros_like(acc)

    lo = cnt_ref[(s * nb1 + t * bpt) * N_EXPERTS + e]
    hi = cnt_ref[(s * nb1 + (t + 1) * bpt) * N_EXPERTS + e]
    shift = st.bit_length() - 1
    k_lo = lax.shift_right_logical(lo, shift)
    k_hi = lax.shift_right_logical(hi + (st - 1), shift)
    slot = slot_ref[...]
    lane = lax.broadcasted_iota(I32, slot.shape, 1)
    col = jnp.sum(jnp.where(lane == e, slot, 0).astype(F32), axis=1, keepdims=True)
    r = lax.broadcasted_iota(I32, (slot.shape[0], st), 1).astype(F32)

    def body(kb, _):
        k0 = pl.multiple_of(kb * st, st)
        onehot = jnp.where(col == r + k0.astype(F32), 1.0, 0.0).astype(BF16)
        acc[...] += _dot(onehot, y_ref[pl.ds(k0, st), :])
        return 0

    lax.fori_loop(k_lo, k_hi, body, 0)

    @pl.when(e == pl.num_programs(2) - 1)
    def _():
        o_ref[...] = x_ref[...] + mod_ref[5:6, :] * acc[...]


def _scatter(slot_col, counts, y, xm, mods, row0, n, nsets, mod_of_set, name):
    cap, blk, nblk, nb1, st = _route_geometry(n)
    tt = _tile(n, 1024)
    nt = n // tt
    cnt = counts[:, :, :N_EXPERTS].reshape(-1)
    grid_spec = pltpu.PrefetchScalarGridSpec(
        num_scalar_prefetch=1,
        grid=(nsets, nt, N_EXPERTS),
        in_specs=[
            pl.BlockSpec((tt, LANES), lambda s, t, e, c: (s * nt + t, 0)),
            pl.BlockSpec((None, cap, D_MODEL), lambda s, t, e, c: (e, s, 0)),
            pl.BlockSpec((tt, D_MODEL), lambda s, t, e, c: (row0 // tt + s * nt + t, 0)),
            pl.BlockSpec((None, 6, D_MODEL), lambda s, t, e, c: (mod_of_set(s), 0, 0)),
        ],
        out_specs=pl.BlockSpec((tt, D_MODEL), lambda s, t, e, c: (s * nt + t, 0)),
        scratch_shapes=[pltpu.VMEM((tt, D_MODEL), F32)],
    )
    return pl.pallas_call(
        functools.partial(_scatter_body, st=st, nb1=nb1, bpt=tt // blk),
        name=name,
        grid_spec=grid_spec,
        out_shape=jax.ShapeDtypeStruct((nsets * n, D_MODEL), F32),
        compiler_params=_cparams("parallel", "parallel", "arbitrary"),
    )(cnt, slot_col, y, xm, mods)


def _moe(xm, g, mods, w_router, w_gate, w_up, w_down, n_lat_rows, bsz, seq, ctx_len, with_ctx):
    h2, probs = _router(xm, g, mods, w_router, n_lat_rows, seq, bsz)

    def route(row0, n, tag):
        slot_col, counts = _topk(probs, row0, n, bsz, "moe_topk" + tag)
        xs, gate = _gather(slot_col, counts, probs, h2, row0, n, bsz, "moe_gather" + tag)
        return slot_col, counts, xs, gate

    slot_l, cnt_l, xs_l, gate_l = route(0, seq, "_lat")
    if with_ctx:
        slot_c, cnt_c, xs_c, gate_c = route(n_lat_rows, ctx_len, "_ctx")
    else:
        xs_c = gate_c = None
    y_l, y_c = _ffn(xs_l, gate_l, xs_c, gate_c, w_gate, w_up, w_down)
    xl = _scatter(slot_l, cnt_l, y_l, xm, mods, 0, seq, bsz, lambda s: s, "moe_scatter_lat")
    xc = None
    if with_ctx:
        xc = _scatter(slot_c, cnt_c, y_c, xm, mods, n_lat_rows, ctx_len, bsz, lambda s: bsz, "moe_scatter_ctx")
    return xl, xc


def _final_norm_body(x_ref, g_ref, o_ref):
    x = x_ref[...]
    o_ref[...] = x * lax.rsqrt(jnp.mean(x * x, axis=-1, keepdims=True) + EPS) * g_ref[...]


def _final_norm(x, g):
    rows, d = x.shape
    tm = _tile(rows, 512)
    return pl.pallas_call(
        _final_norm_body,
        name="final_norm",
        grid=(rows // tm,),
        in_specs=[pl.BlockSpec((tm, d), lambda i: (i, 0)), pl.BlockSpec((1, d), lambda i: (0, 0))],
        out_specs=pl.BlockSpec((tm, d), lambda i: (i, 0)),
        out_shape=jax.ShapeDtypeStruct((rows, d), F32),
        compiler_params=_cparams("parallel"),
    )(x, g)


def _reorder_w_in(w):
    pad = jnp.zeros((w.shape[0], P_WIDTH - P_SMALL - N_SMALL), w.dtype)
    return jnp.concatenate([w[:, 0:3584], w[:, 3600:5648], w[:, 5664:13856], w[:, 3584:3600], w[:, 5648:5664], pad],
                           axis=1).astype(BF16)


def _mixer(xl, xc, mods, lw, tabs_lat, tabs_ctx, bsz, seq, ctx_len, with_ctx_out):
    n_lat_rows = bsz * seq
    rows_all = n_lat_rows + bsz * ctx_len
    rows = rows_all if with_ctx_out else n_lat_rows
    h = _normmod(xl, xc, lw["norm_mix"], mods, seq, bsz)
    p = _matmul(h, lw["w_in"], 1024, 1024, "inproj")
    small_t = _small_t(p)
    ones = lambda n: jnp.ones((1, n), F32)
    u_hy = _short_conv(p, P_HY, 3 * W, lw["hy_conv_w"], lw["hy_conv_b"].reshape(1, -1), ones(3 * W), False,
                       n_lat_rows, ctx_len, "conv_hyena")
    hy_args = (lw["hy_w1"], lw["hy_b1"], lw["hy_fr1"], lw["hy_w2"], lw["hy_b2"], lw["hy_fr2"], lw["hy_w3"],
               lw["hy_decay"])

    def hy_filter(n, tabs, tag):
        gp, gm = _hy_filters(n, *hy_args, "hy_filter" + tag)
        return (_matmul(tabs[0], gp, 512, 512, "hy_spec_c" + tag), _matmul(tabs[1], gm, 512, 512, "hy_spec_s" + tag),
                lw["hy_bias"])

    hy_lat = _hyena(u_hy, 0, seq, bsz, tabs_lat, hy_filter(seq, tabs_lat, "_lat"), "_lat")
    hy_ctx = hy_lat
    if with_ctx_out:
        hy_ctx = _hyena(u_hy, n_lat_rows, ctx_len, bsz, tabs_ctx, hy_filter(ctx_len, tabs_ctx, "_ctx"), "_ctx")
    post = jnp.concatenate([jnp.ones((1, W), F32), jnp.full((1, W), ML_HD ** -0.5, F32)], axis=1)
    qk = _short_conv(p, P_MLQK, 2 * W, lw["ml_conv_w"], lw["ml_conv_b"].reshape(1, -1), post, True, n_lat_rows,
                     ctx_len, "conv_mlstm")
    hf, hb = _mlstm_scan(qk, p, small_t, lw["ml_gate_b"].reshape(16), bsz, seq, ctx_len)
    y_ml = _mlstm_finish(hf, hb, p, lw["ml_norm"].reshape(1, W), rows)
    y_s5 = _s5_branch(p, rows, bsz, seq, ctx_len, lw["s5_a_re"], lw["s5_a_im"], lw["s5_log_dt"], lw["s5_b_re"],
                      lw["s5_b_im"], lw["s5_c_re"], lw["s5_c_im"], lw["s5_d"], lw["s5_glu_w"], lw["s5_glu_b"])
    xbc = _short_conv(p, P_XBC, 2 * W, lw["ssd_conv_w"], lw["ssd_conv_b"].reshape(1, -1), ones(2 * W), True,
                      n_lat_rows, ctx_len, "conv_ssd")
    sf, sb = _ssd_scan(xbc, p, small_t, lw["ssd_dt_bias"].reshape(16), lw["ssd_a_log"].reshape(16), bsz, seq, ctx_len)
    y_ssd = _ssd_finish(sf, sb, xbc, p, lw["ssd_d"], lw["ssd_norm"].reshape(1, W), rows)
    acc = _merge(hy_lat, hy_ctx, (y_ml, y_s5, y_ssd), p, lw["w_branch"], rows)
    return _outproj(acc, lw["w_out"], xl, xc, mods, seq, bsz)


_PER_LAYER = ("ada_w", "ada_b", "norm_mix", "norm_ffn", "w_in", "hy_conv_w", "hy_conv_b", "hy_w1", "hy_b1", "hy_fr1",
              "hy_w2", "hy_b2", "hy_fr2", "hy_w3", "hy_decay", "hy_bias", "ml_conv_w", "ml_conv_b", "ml_gate_b",
              "ml_norm", "s5_a_re", "s5_a_im", "s5_log_dt", "s5_b_re", "s5_b_im", "s5_c_re", "s5_c_im", "s5_d",
              "s5_glu_w", "s5_glu_b", "ssd_conv_w", "ssd_conv_b", "ssd_dt_bias", "ssd_a_log", "ssd_d", "ssd_norm",
              "w_branch", "w_out", "w_router", "w_gate", "w_up", "w_down")


def kernel(x, c, ctx, c_ctx, ada_w, ada_b, norm_mix, norm_ffn, w_in, hy_conv_w, hy_conv_b, hy_w1, hy_b1, hy_fr1, hy_w2, hy_b2, hy_fr2, hy_w3, hy_decay, hy_bias, ml_conv_w, ml_conv_b, ml_gate_b, ml_norm, s5_a_re, s5_a_im, s5_log_dt, s5_b_re, s5_b_im, s5_c_re, s5_c_im, s5_d, s5_glu_w, s5_glu_b, ssd_conv_w, ssd_conv_b, ssd_dt_bias, ssd_a_log, ssd_d, ssd_norm, w_branch, w_out, w_router, w_gate, w_up, w_down, final_norm):
    stacked = dict(zip(_PER_LAYER, (ada_w, ada_b, norm_mix, norm_ffn, w_in, hy_conv_w, hy_conv_b, hy_w1, hy_b1, hy_fr1,
                                    hy_w2, hy_b2, hy_fr2, hy_w3, hy_decay, hy_bias, ml_conv_w, ml_conv_b, ml_gate_b,
                                    ml_norm, s5_a_re, s5_a_im, s5_log_dt, s5_b_re, s5_b_im, s5_c_re, s5_c_im, s5_d,
                                    s5_glu_w, s5_glu_b, ssd_conv_w, ssd_conv_b, ssd_dt_bias, ssd_a_log, ssd_d, ssd_norm,
                                    w_branch, w_out, w_router, w_gate, w_up, w_down)))
    bsz, seq, d = x.shape
    ctx_len = ctx.shape[1]
    depth = ada_w.shape[0]
    assert d == D_MODEL and 2 * bsz <= 8 and seq % ctx_len == 0 and ctx_len % SCAN_BLOCK == 0
    xl = x.reshape(bsz * seq, d)
    xc = ctx.reshape(bsz * ctx_len, d)
    cc = jnp.zeros((8, d), F32).at[:bsz].set(c).at[bsz].set(c_ctx)
    tabs_lat = _dft_tables(seq)
    tabs_ctx = _dft_tables(ctx_len)
    for i in range(depth):
        last = i == depth - 1
        lw = {k: v[i] for k, v in stacked.items()}
        lw["w_in"] = _reorder_w_in(lw["w_in"])
        lw["norm_mix"] = lw["norm_mix"].reshape(1, d)
        lw["norm_ffn"] = lw["norm_ffn"].reshape(1, d)
        lw["w_branch"] = lw["w_branch"].astype(BF16)
        lw["w_out"] = lw["w_out"].astype(BF16)
        mods = _ada(cc, lw["ada_w"], lw["ada_b"].reshape(1, -1)).reshape(8, 6, d)
        xm = _mixer(xl, xc, mods, lw, tabs_lat, tabs_ctx, bsz, seq, ctx_len, not last)
        xl, xc_new = _moe(xm, lw["norm_ffn"], mods, lw["w_router"], lw["w_gate"], lw["w_up"], lw["w_down"],
                          bsz * seq, bsz, seq, ctx_len, not last)
        if not last:
            xc = xc_new
    return _final_norm(xl, final_norm.reshape(1, d)).reshape(bsz, seq, d)
```

```python
import functools
import math

import jax
import jax.numpy as jnp
from jax import lax
from jax.experimental import pallas as pl
from jax.experimental.pallas import tpu as pltpu

F32 = jnp.float32
BF16 = jnp.bfloat16
I32 = jnp.int32

D_MODEL = 2048
W = 512
GRID_W = 64
EPS = 1e-6
HY_EMB = 33
HY_BANDS = 16
HY_FFN = 64
ML_HEADS = 4
ML_HD = 128
ML_CHUNK = 64
S5_GROUP = 16
S5_GROUPS = 32
S5_STATE = 64
S5_NS = S5_GROUPS * S5_STATE
S5_SUPER = 4
S5_T = 64
SSD_HD = 64
SSD_HEADS = 8
SSD_GROUPS = 2
SSD_STATE = 128
SSD_CHUNK = 128
N_EXPERTS = 16
EC_CAPACITY = 2
D_EXPERT = 1536
SCAN_BLOCK = 128
ROUTE_BLOCK = 256

P_HY = 0
P_MLQK = 1536
P_MLV = 2560
P_MLO = 3072
P_S5 = 3584
P_SSDZ = 4096
P_XBC = 4608
P_GATE = 5632
P_SMALL = 13824
P_WIDTH = 14336
N_SMALL = 32

LANES = 128
VMEM_LIMIT_BYTES = 56 * 1024 * 1024


def _cparams(*sem):
    return pltpu.CompilerParams(dimension_semantics=sem, vmem_limit_bytes=VMEM_LIMIT_BYTES)


def _dot(a, b):
    return jnp.dot(a, b, preferred_element_type=F32)


def _dot_nt(a, b):
    return lax.dot_general(a, b, (((1,), (1,)), ((), ())), preferred_element_type=F32)


def _split3(x):
    hi = x.astype(BF16)
    r = x - hi.astype(F32)
    mid = r.astype(BF16)
    lo = (r - mid.astype(F32)).astype(BF16)
    return hi, mid, lo


def _dot_sel(m01, x):
    hi, mid, lo = _split3(x)
    return _dot(m01, hi) + _dot(m01, mid) + _dot(m01, lo)


def _sel_dot(x, m01):
    hi, mid, lo = _split3(x)
    return _dot(hi, m01) + _dot(mid, m01) + _dot(lo, m01)


def _dot3(a, b):
    ah = a.astype(BF16)
    al = (a - ah.astype(F32)).astype(BF16)
    bh = b.astype(BF16)
    bl = (b - bh.astype(F32)).astype(BF16)
    return _dot(ah, bh) + _dot(al, bh) + _dot(ah, bl)


def _sigmoid(x):
    return 1.0 / (1.0 + jnp.exp(-x))


def _silu(x):
    return x * _sigmoid(x)


def _softplus(x):
    return jnp.maximum(x, 0.0) + jnp.log(1.0 + jnp.exp(-jnp.abs(x)))


def _log_sigmoid(x):
    return jnp.minimum(x, 0.0) - jnp.log(1.0 + jnp.exp(-jnp.abs(x)))


def _gelu_tanh(x):
    return 0.5 * x * (1.0 + jnp.tanh(math.sqrt(2.0 / math.pi) * (x + 0.044715 * (x * x * x))))


def _tile(n, pref):
    t = min(n, pref)
    while n % t:
        t //= 2
    return t


def _mod_index(i, tm, n_lat_tiles, seq, bsz):
    return jnp.where(i < n_lat_tiles, (i * tm) // seq, bsz)


def _mm_body(a_ref, b_ref, o_ref):
    o_ref[...] = _dot(a_ref[...].astype(BF16), b_ref[...].astype(BF16)).astype(o_ref.dtype)


def _matmul(a, b, tm, tn, name, out_dtype=F32):
    m, k = a.shape
    n = b.shape[1]
    tm, tn = _tile(m, tm), _tile(n, tn)
    return pl.pallas_call(
        _mm_body,
        name=name,
        grid=(m // tm, n // tn),
        in_specs=[pl.BlockSpec((tm, k), lambda i, j: (i, 0)), pl.BlockSpec((k, tn), lambda i, j: (0, j))],
        out_specs=pl.BlockSpec((tm, tn), lambda i, j: (i, j)),
        out_shape=jax.ShapeDtypeStruct((m, n), out_dtype),
        compiler_params=_cparams("parallel", "parallel"),
    )(a, b)


def _ada_body(c_ref, w_ref, b_ref, o_ref):
    c = c_ref[...]
    o_ref[...] = _dot3(_silu(c), w_ref[...]) + b_ref[...]


def _ada(cc, w, b, layer):
    _, d, n = w.shape
    tn = 1024
    return pl.pallas_call(
        _ada_body,
        name="ada_mod",
        grid=(n // tn,),
        in_specs=[pl.BlockSpec((8, d), lambda j: (0, 0)), pl.BlockSpec((None, d, tn), lambda j: (layer, 0, j)),
                  pl.BlockSpec((1, tn), lambda j: (0, j))],
        out_specs=pl.BlockSpec((8, tn), lambda j: (0, j)),
        out_shape=jax.ShapeDtypeStruct((8, n), F32),
        compiler_params=_cparams("parallel"),
    )(cc, w, b)


def _norm_mod(x, g, shift, scale):
    y = x * lax.rsqrt(jnp.mean(x * x, axis=-1, keepdims=True) + EPS) * g
    return y * (1.0 + scale) + shift


def _normmod_body(xl_ref, xc_ref, g_ref, mod_ref, h_ref, *, n_lat_tiles):
    i = pl.program_id(0)

    @pl.when(i < n_lat_tiles)
    def _():
        h_ref[...] = _norm_mod(xl_ref[...], g_ref[...], mod_ref[0:1, :], mod_ref[1:2, :]).astype(BF16)

    @pl.when(i >= n_lat_tiles)
    def _():
        h_ref[...] = _norm_mod(xc_ref[...], g_ref[...], mod_ref[0:1, :], mod_ref[1:2, :]).astype(BF16)


def _normmod(xl, xc, g, mods, seq, bsz):
    nl, nc = xl.shape[0], xc.shape[0]
    d = xl.shape[1]
    tm = _tile(math.gcd(nl, nc), 512)
    n_lat = nl // tm
    return pl.pallas_call(
        functools.partial(_normmod_body, n_lat_tiles=n_lat),
        name="mixer_normmod",
        grid=((nl + nc) // tm,),
        in_specs=[
            pl.BlockSpec((tm, d), lambda i: (jnp.minimum(i, n_lat - 1), 0)),
            pl.BlockSpec((tm, d), lambda i: (jnp.maximum(i - n_lat, 0), 0)),
            pl.BlockSpec((1, d), lambda i: (0, 0)),
            pl.BlockSpec((None, 6, d), lambda i: (_mod_index(i, tm, n_lat, seq, bsz), 0, 0)),
        ],
        out_specs=pl.BlockSpec((tm, d), lambda i: (i, 0)),
        out_shape=jax.ShapeDtypeStruct((nl + nc, d), BF16),
        compiler_params=_cparams("parallel"),
    )(xl, xc, g, mods)


def _small_t_body(p_ref, o_ref):
    o_ref[...] = p_ref[...].T[0:N_SMALL, :]


def _small_t(p):
    rows = p.shape[0]
    tm = _tile(rows, 512)
    return pl.pallas_call(
        _small_t_body,
        name="small_transpose",
        grid=(rows // tm,),
        in_specs=[pl.BlockSpec((tm, LANES), lambda i: (i, P_SMALL // LANES))],
        out_specs=pl.BlockSpec((N_SMALL, tm), lambda i: (0, i)),
        out_shape=jax.ShapeDtypeStruct((N_SMALL, rows), F32),
        compiler_params=_cparams("parallel"),
    )(p)


def _conv_body(u_ref, w_ref, b_ref, s_ref, o_ref, *, act, n_lat_tiles, tile, ctx_len):
    i = pl.program_id(0)
    u = u_ref[...]
    t = lax.broadcasted_iota(I32, u.shape, 0)
    pos = jnp.where(i < n_lat_tiles, t & (GRID_W - 1), lax.rem(t, ctx_len))
    last = jnp.where(i < n_lat_tiles, GRID_W - 1, ctx_len - 1)
    prev = jnp.where(pos == 0, 0.0, pltpu.roll(u, 1, 0))
    nxt = jnp.where(pos == last, 0.0, pltpu.roll(u, tile - 1, 0))
    y = w_ref[0:1, :] * prev + w_ref[1:2, :] * u + w_ref[2:3, :] * nxt + b_ref[...]
    if act:
        y = _silu(y)
    o_ref[...] = y * s_ref[...]


def _short_conv(p, col0, width, w, b, post, act, n_lat_rows, ctx_len, name):
    rows = p.shape[0]
    tile = _tile(math.gcd(n_lat_rows, rows - n_lat_rows), 1024)
    assert tile % ctx_len == 0 and tile % GRID_W == 0
    cb = 512
    return pl.pallas_call(
        functools.partial(_conv_body, act=act, n_lat_tiles=n_lat_rows // tile, tile=tile, ctx_len=ctx_len),
        name=name,
        grid=(rows // tile, width // cb),
        in_specs=[
            pl.BlockSpec((tile, cb), lambda i, j: (i, col0 // cb + j)),
            pl.BlockSpec((3, cb), lambda i, j: (0, j)),
            pl.BlockSpec((1, cb), lambda i, j: (0, j)),
            pl.BlockSpec((1, cb), lambda i, j: (0, j)),
        ],
        out_specs=pl.BlockSpec((tile, cb), lambda i, j: (i, j)),
        out_shape=jax.ShapeDtypeStruct((rows, width), F32),
        compiler_params=_cparams("parallel", "parallel"),
    )(p, w, b, post)


def _hy_filter_body(feat_ref, w1_ref, b1_ref, fr1_ref, w2_ref, b2_ref, fr2_ref, w3a_ref, w3b_ref,
                    da_ref, db_ref, gp_ref, gm_ref, *, seq):
    hdn = jnp.sin(fr1_ref[...] * (_dot3(feat_ref[...], w1_ref[...]) + b1_ref[...]))
    hdn = jnp.sin(fr2_ref[...] * (_dot3(hdn, w2_ref[...]) + b2_ref[...]))
    t = lax.broadcasted_iota(I32, (seq, 1), 0)
    tn = t.astype(F32) / float(seq - 1)
    hf = _dot3(hdn, w3a_ref[...]) * jnp.exp(-tn * jnp.abs(da_ref[...]))
    hb = _dot3(hdn, w3b_ref[...]) * jnp.exp(-tn * jnp.abs(db_ref[...]))
    norm = jnp.sum(jnp.abs(hf) + jnp.abs(hb), axis=0, keepdims=True)
    hf = hf / norm
    hb = jnp.where(t == 0, 0.0, hb / norm)
    gp_ref[...] = hf + hb
    gm_ref[...] = hf - hb


def _hy_filters(seq, w1, b1, fr1, w2, b2, fr2, w3, decay, name):
    t = jnp.arange(seq, dtype=F32)
    freqs = jnp.linspace(1e-4, HY_BANDS - 1, HY_BANDS, dtype=F32)
    ang = (2.0 * math.pi / seq) * t[:, None] * freqs[None, :]
    feats = jnp.concatenate([(t / (seq - 1))[:, None], jnp.cos(ang), -jnp.sin(ang)], axis=-1)
    hp = LANES
    feats = jnp.pad(feats, ((0, 0), (0, hp - HY_EMB)))
    w1p = jnp.pad(w1, ((0, hp - HY_EMB), (0, hp - HY_FFN)))
    w2p = jnp.pad(w2, ((0, hp - HY_FFN), (0, hp - HY_FFN)))
    w3p = jnp.pad(w3, ((0, hp - HY_FFN), (0, 0))).reshape(hp, 4, W).transpose(1, 0, 2)
    row = lambda v: jnp.pad(v, (0, hp - HY_FFN)).reshape(1, hp)
    dec = decay.reshape(4, 1, W)
    cb = 256
    full = lambda shape: pl.BlockSpec(shape, lambda o, j: (0,) * len(shape))
    gp, gm = pl.pallas_call(
        functools.partial(_hy_filter_body, seq=seq),
        name=name,
        grid=(2, W // cb),
        in_specs=[
            full((seq, hp)), full((hp, hp)), full((1, hp)), full((1, hp)),
            full((hp, hp)), full((1, hp)), full((1, hp)),
            pl.BlockSpec((None, hp, cb), lambda o, j: (2 * o, 0, j)),
            pl.BlockSpec((None, hp, cb), lambda o, j: (2 * o + 1, 0, j)),
            pl.BlockSpec((None, 1, cb), lambda o, j: (2 * o, 0, j)),
            pl.BlockSpec((None, 1, cb), lambda o, j: (2 * o + 1, 0, j)),
        ],
        out_specs=[pl.BlockSpec((seq, cb), lambda o, j: (0, o * (W // cb) + j))] * 2,
        out_shape=[jax.ShapeDtypeStruct((seq, 2 * W), F32)] * 2,
        compiler_params=_cparams("parallel", "parallel"),
    )(feats, w1p, row(b1), row(fr1), w2p, row(b2), row(fr2), w3p, w3p, dec, dec)
    return gp, gm


DFT_RADIX = 64


def _dft_body(ca_ref, sa_ref, cb_ref, sb_ref, ea_ref, eb_ref, c_ref, s_ref):
    ea, eb = ea_ref[...], eb_ref[...]
    ca, sa = _sel_dot(ca_ref[...], ea), _sel_dot(sa_ref[...], ea)
    cb, sb = _sel_dot(cb_ref[...], eb), _sel_dot(sb_ref[...], eb)
    c_ref[...] = (ca * cb - sa * sb).astype(BF16)
    s_ref[...] = (sa * cb + ca * sb).astype(BF16)


def _dft_tables(seq):
    rdx = DFT_RADIX
    r = jnp.arange(seq, dtype=I32)[:, None]
    a = jnp.arange(rdx, dtype=I32)[None, :]
    col = jnp.arange(seq, dtype=I32)[None, :]
    ea = jnp.where(col // rdx == a.T, 1.0, 0.0).astype(BF16)
    eb = jnp.where(col % rdx == a.T, 1.0, 0.0).astype(BF16)

    def small(m):
        ang = (m % (4 * seq)).astype(F32) * (math.pi / (2 * seq))
        return jnp.cos(ang), jnp.sin(ang)

    def build(ma, mb, name):
        (ca, sa), (cb, sb) = small(ma), small(mb)
        tr = _tile(seq, 256)
        sm = pl.BlockSpec((tr, rdx), lambda i: (i, 0))
        ex = pl.BlockSpec((rdx, seq), lambda i: (0, 0))
        return pl.pallas_call(
            _dft_body,
            name=name,
            grid=(seq // tr,),
            in_specs=[sm, sm, sm, sm, ex, ex],
            out_specs=[pl.BlockSpec((tr, seq), lambda i: (i, 0))] * 2,
            out_shape=[jax.ShapeDtypeStruct((seq, seq), BF16)] * 2,
            compiler_params=_cparams("parallel"),
        )(ca, sa, cb, sb, ea, eb)

    c, s = build((2 * r + 1) * (rdx * a), (2 * r + 1) * a, "dft_table")
    ct, st = build(r * (2 * rdx * a), r * (2 * a + 1), "dft_table_t")
    return c, s, ct, st


def _hy_fwd_body(c_ref, s_ref, z_ref, gc_ref, gs_ref, p1_ref, p2_ref):
    z = z_ref[...].astype(BF16)
    zc = _dot(c_ref[...], z)
    zs = _dot(s_ref[...], z)
    gc, gs = gc_ref[...], gs_ref[...]
    p1_ref[...] = (zc * gc - zs * gs).astype(BF16)
    p2_ref[...] = (zc * gs + zs * gc).astype(BF16)


def _hy_fwd(ctab, stab, z, zcol, zrow0, gc, gs, gcol, seq, nseq, name):
    tk = _tile(seq, 512)
    nk = seq // tk
    zb0 = zrow0 // seq
    return pl.pallas_call(
        _hy_fwd_body,
        name=name,
        grid=(nseq, nk),
        in_specs=[
            pl.BlockSpec((tk, seq), lambda b, k: (k, 0)),
            pl.BlockSpec((tk, seq), lambda b, k: (k, 0)),
            pl.BlockSpec((seq, W), lambda b, k: (zb0 + b, zcol)),
            pl.BlockSpec((tk, W), lambda b, k: (k, gcol)),
            pl.BlockSpec((tk, W), lambda b, k: (k, gcol)),
        ],
        out_specs=[pl.BlockSpec((tk, W), lambda b, k: (b * nk + k, 0))] * 2,
        out_shape=[jax.ShapeDtypeStruct((nseq * seq, W), BF16)] * 2,
        compiler_params=_cparams("parallel", "arbitrary"),
    )(ctab, stab, z, gc, gs)


def _hy_inv_body(ct_ref, st_ref, p1_ref, p2_ref, zin_ref, mul_ref, bias_ref, o_ref, *, seq):
    y = (_dot(ct_ref[...], p1_ref[...]) + _dot(st_ref[...], p2_ref[...])) * (1.0 / seq)
    o_ref[...] = mul_ref[...] * (y + bias_ref[...] * zin_ref[...])


def _hy_inv(cttab, sttab, p1, p2, zin, zin_col, zin_row0, mul, mul_col, mul_row0, bias, seq, nseq, name):
    tt = _tile(seq, 512)
    nt = seq // tt
    zr, mr = zin_row0 // tt, mul_row0 // tt
    return pl.pallas_call(
        functools.partial(_hy_inv_body, seq=seq),
        name=name,
        grid=(nseq, nt),
        in_specs=[
            pl.BlockSpec((tt, seq), lambda b, t: (t, 0)),
            pl.BlockSpec((tt, seq), lambda b, t: (t, 0)),
            pl.BlockSpec((seq, W), lambda b, t: (b, 0)),
            pl.BlockSpec((seq, W), lambda b, t: (b, 0)),
            pl.BlockSpec((tt, W), lambda b, t: (zr + b * nt + t, zin_col)),
            pl.BlockSpec((tt, W), lambda b, t: (mr + b * nt + t, mul_col)),
            pl.BlockSpec((1, W), lambda b, t: (0, 0)),
        ],
        out_specs=pl.BlockSpec((tt, W), lambda b, t: (b * nt + t, 0)),
        out_shape=jax.ShapeDtypeStruct((nseq * seq, W), F32),
        compiler_params=_cparams("parallel", "arbitrary"),
    )(cttab, sttab, p1, p2, zin, mul, bias)


def _hyena(u, row0, seq, nseq, tabs, filt, tag):
    ctab, stab, cttab, sttab = tabs
    gc, gs, bias = filt
    p1, p2 = _hy_fwd(ctab, stab, u, 0, row0, gc, gs, 0, seq, nseq, "hy_fwd1" + tag)
    z2 = _hy_inv(cttab, sttab, p1, p2, u, 0, row0, u, 1, row0, bias[0:1], seq, nseq, "hy_inv1" + tag)
    p1, p2 = _hy_fwd(ctab, stab, z2, 0, 0, gc, gs, 1, seq, nseq, "hy_fwd2" + tag)
    return _hy_inv(cttab, sttab, p1, p2, z2, 0, 0, u, 2, row0, bias[1:2], seq, nseq, "hy_inv2" + tag)


def _chunk_index(b, c, rev, n_ctx_chunks, n_lat_chunks, bsz):
    in_ctx = c < n_ctx_chunks
    if rev:
        cc = n_ctx_chunks - 1 - c
        lc = n_lat_chunks - 1 - (c - n_ctx_chunks)
    else:
        cc = c
        lc = c - n_ctx_chunks
    return jnp.where(in_ctx, bsz * n_lat_chunks + b * n_ctx_chunks + cc, b * n_lat_chunks + lc)


def _tri(n, rev):
    r = lax.broadcasted_iota(I32, (n, n), 0)
    c = lax.broadcasted_iota(I32, (n, n), 1)
    return (c >= r) if rev else (c <= r)


def _mlstm_chunk(q, k, v, ig_col, ig_row, b_col, b_row, ct, nrow, m, mask, last):
    a_col = b_col + m
    dmat = jnp.where(mask, b_col - b_row + ig_row, -jnp.inf)
    mt = jnp.maximum(a_col, jnp.max(dmat, axis=1, keepdims=True))
    inter = jnp.exp(a_col - mt)
    qb, kb = q.astype(BF16), k.astype(BF16)
    s = _dot_nt(qb, kb) * jnp.exp(dmat - mt)
    num = _dot(s.astype(BF16), v.astype(BF16)) + inter * _dot(qb, ct.astype(BF16))
    den = jnp.sum(s, axis=1, keepdims=True) + inter * jnp.sum(q * nrow, axis=1, keepdims=True)
    h = num / jnp.maximum(jnp.abs(den), jnp.exp(-mt))
    m_new = mt[last:last + 1, :]
    tot = b_col[last:last + 1, :]
    ws = jnp.exp(tot - b_col + ig_col - m_new)
    dec = jnp.exp(tot + m - m_new)
    ct_new = dec * ct + _dot(k.T.astype(BF16), (v * ws).astype(BF16))
    n_new = dec * nrow + jnp.sum(k * ws, axis=0, keepdims=True)
    return h, ct_new, n_new, m_new


def _mlstm_body(qkf_ref, vf_ref, gcf_ref, grf_ref, qkb_ref, vb_ref, gcb_ref, grb_ref, gbc_ref, gbr_ref,
                hf_ref, hb_ref, ct_scr, n_scr, m_scr):
    T = ML_CHUNK

    @pl.when(pl.program_id(1) == 0)
    def _():
        ct_scr[...] = jnp.zeros_like(ct_scr)
        n_scr[...] = jnp.zeros_like(n_scr)
        m_scr[...] = jnp.zeros_like(m_scr)

    dirs = ((qkf_ref, vf_ref, gcf_ref, grf_ref, hf_ref), (qkb_ref, vb_ref, gcb_ref, grb_ref, hb_ref))
    for d, (qk_ref, v_ref, gc_ref, gr_ref, h_ref) in enumerate(dirs):
        rev = d == 1
        mask = _tri(T, rev)
        m01 = jnp.where(mask, 1.0, 0.0).astype(BF16)
        m01t = jnp.where(_tri(T, not rev), 1.0, 0.0).astype(BF16)
        gcol = gc_ref[:, 0:16] + gbc_ref[...]
        grow = gr_ref[0:16, :] + gbr_ref[...]
        ls_col, ls_row = _log_sigmoid(gcol), _log_sigmoid(grow)
        last = 0 if rev else T - 1
        nsub = SCAN_BLOCK // T
        carry = [(ct_scr[d * ML_HEADS + h], n_scr[d * ML_HEADS + h:d * ML_HEADS + h + 1, :],
                  m_scr[d * ML_HEADS + h:d * ML_HEADS + h + 1, 0:1]) for h in range(ML_HEADS)]
        for sub in (range(nsub - 1, -1, -1) if rev else range(nsub)):
            rs = slice(sub * T, (sub + 1) * T)
            cum_col = _dot_sel(m01, ls_col[rs])
            cum_row = _sel_dot(ls_row[:, rs], m01t)
            for h in range(ML_HEADS):
                ci, cf = 2 * d * ML_HEADS + h, (2 * d + 1) * ML_HEADS + h
                hs = slice(h * ML_HD, (h + 1) * ML_HD)
                ks = slice(W + h * ML_HD, W + (h + 1) * ML_HD)
                ct, nrow, m = carry[h]
                hh, ct, nrow, m = _mlstm_chunk(
                    qk_ref[rs, hs], qk_ref[rs, ks], v_ref[rs, hs],
                    gcol[rs, ci:ci + 1], grow[ci:ci + 1, rs], cum_col[:, cf:cf + 1], cum_row[cf:cf + 1, :],
                    ct, nrow, m, mask, last)
                h_ref[rs, hs] = hh
                carry[h] = (ct, nrow, m)
        for h in range(ML_HEADS):
            ct, nrow, m = carry[h]
            j = d * ML_HEADS + h
            ct_scr[j] = ct
            n_scr[j:j + 1, :] = nrow
            m_scr[j:j + 1, :] = jnp.broadcast_to(m, (1, LANES))


def _mlstm_scan(qk, p, small_t, gate_b, bsz, seq, ctx_len):
    T = SCAN_BLOCK
    rows = qk.shape[0]
    ncc, nlc = ctx_len // T, seq // T
    cf = functools.partial(_chunk_index, rev=False, n_ctx_chunks=ncc, n_lat_chunks=nlc, bsz=bsz)
    cr = functools.partial(_chunk_index, rev=True, n_ctx_chunks=ncc, n_lat_chunks=nlc, bsz=bsz)

    def specs(ci):
        return [pl.BlockSpec((T, 2 * W), lambda b, c: (ci(b, c), 0)),
                pl.BlockSpec((T, W), lambda b, c: (ci(b, c), P_MLV // W)),
                pl.BlockSpec((T, LANES), lambda b, c: (ci(b, c), P_SMALL // LANES)),
                pl.BlockSpec((N_SMALL, T), lambda b, c: (0, ci(b, c)))]

    nchain = 2 * ML_HEADS
    return pl.pallas_call(
        _mlstm_body,
        name="mlstm_scan",
        grid=(bsz, ncc + nlc),
        in_specs=specs(cf) + specs(cr) + [pl.BlockSpec((1, 16), lambda b, c: (0, 0)),
                                         pl.BlockSpec((16, 1), lambda b, c: (0, 0))],
        out_specs=[pl.BlockSpec((T, W), lambda b, c: (cf(b, c), 0)), pl.BlockSpec((T, W), lambda b, c: (cr(b, c), 0))],
        out_shape=[jax.ShapeDtypeStruct((rows, W), F32)] * 2,
        scratch_shapes=[pltpu.VMEM((nchain, ML_HD, ML_HD), F32), pltpu.VMEM((nchain, LANES), F32),
                        pltpu.VMEM((nchain, LANES), F32)],
        compiler_params=_cparams("parallel", "arbitrary"),
    )(qk, p, p, small_t, qk, p, p, small_t, gate_b.reshape(1, 16), gate_b.reshape(16, 1))


def _mlstm_fin_body(hf_ref, hb_ref, o_ref, g_ref, y_ref):
    h = hf_ref[...] + hb_ref[...]
    for i in range(ML_HEADS):
        hh = h[:, i * ML_HD:(i + 1) * ML_HD]
        hh = hh * lax.rsqrt(jnp.mean(hh * hh, axis=-1, keepdims=True) + EPS)
        sl = slice(i * ML_HD, (i + 1) * ML_HD)
        y_ref[:, sl] = hh * g_ref[:, sl] * _sigmoid(o_ref[:, sl])


def _mlstm_finish(hf, hb, p, norm_g, rows):
    tm = _tile(rows, 512)
    return pl.pallas_call(
        _mlstm_fin_body,
        name="mlstm_finish",
        grid=(rows // tm,),
        in_specs=[pl.BlockSpec((tm, W), lambda i: (i, 0)), pl.BlockSpec((tm, W), lambda i: (i, 0)),
                  pl.BlockSpec((tm, W), lambda i: (i, P_MLO // W)), pl.BlockSpec((1, W), lambda i: (0, 0))],
        out_specs=pl.BlockSpec((tm, W), lambda i: (i, 0)),
        out_shape=jax.ShapeDtypeStruct((rows, W), F32),
        compiler_params=_cparams("parallel"),
    )(hf, hb, p, norm_g)


def _ssd_body(xf_ref, dcf_ref, drf_ref, xb_ref, dcb_ref, drb_ref, dbc_ref, dbr_ref, ac_ref, ar_ref,
              yf_ref, yb_ref, st_scr):
    T = SSD_CHUNK

    @pl.when(pl.program_id(1) == 0)
    def _():
        st_scr[...] = jnp.zeros_like(st_scr)

    for d, (xbc_ref, dcol_ref, drow_ref, y_ref) in enumerate(((xf_ref, dcf_ref, drf_ref, yf_ref),
                                                              (xb_ref, dcb_ref, drb_ref, yb_ref))):
        rev = d == 1
        mask = _tri(T, rev)
        m01 = jnp.where(mask, 1.0, 0.0).astype(BF16)
        m01t = jnp.where(_tri(T, not rev), 1.0, 0.0).astype(BF16)
        dt_col = _softplus(dcol_ref[:, 16:32] + dbc_ref[...])
        dt_row = _softplus(drow_ref[16:32, :] + dbr_ref[...])
        acs_col = _dot_sel(m01, dt_col * ac_ref[...])
        acs_row = _sel_dot(dt_row * ar_ref[...], m01t)
        last = 0 if rev else T - 1
        for g in range(SSD_GROUPS):
            bm = xbc_ref[:, W + g * SSD_STATE:W + (g + 1) * SSD_STATE]
            cm = xbc_ref[:, W + (SSD_GROUPS + g) * SSD_STATE:W + (SSD_GROUPS + g + 1) * SSD_STATE]
            bmb, cmb = bm.astype(BF16), cm.astype(BF16)
            cb = _dot_nt(cmb, bmb)
            bmt = bm.T.astype(BF16)
            for hh in range(SSD_HEADS // SSD_GROUPS):
                h = g * (SSD_HEADS // SSD_GROUPS) + hh
                ci = d * SSD_HEADS + h
                a_col, a_row = acs_col[:, ci:ci + 1], acs_row[ci:ci + 1, :]
                x = xbc_ref[:, h * SSD_HD:(h + 1) * SSD_HD] * dt_col[:, ci:ci + 1]
                lm = jnp.where(mask, jnp.exp(jnp.where(mask, a_col - a_row, 0.0)), 0.0)
                st = st_scr[ci]
                y = _dot((cb * lm).astype(BF16), x.astype(BF16)) + _dot(cmb, st.astype(BF16)) * jnp.exp(a_col)
                y_ref[:, h * SSD_HD:(h + 1) * SSD_HD] = y
                tot = a_col[last:last + 1, :]
                xd = x * jnp.exp(tot - a_col)
                st_scr[ci] = jnp.exp(tot) * st + _dot(bmt, xd.astype(BF16))


def _ssd_scan(xbc, p, small_t, dt_bias, a_log, bsz, seq, ctx_len):
    T = SSD_CHUNK
    rows = xbc.shape[0]
    ncc, nlc = ctx_len // T, seq // T
    cf = functools.partial(_chunk_index, rev=False, n_ctx_chunks=ncc, n_lat_chunks=nlc, bsz=bsz)
    cr = functools.partial(_chunk_index, rev=True, n_ctx_chunks=ncc, n_lat_chunks=nlc, bsz=bsz)
    a = -jnp.exp(a_log.astype(F32))
    small = lambda shape: pl.BlockSpec(shape, lambda b, c: (0, 0))

    def specs(ci):
        return [pl.BlockSpec((T, 2 * W), lambda b, c: (ci(b, c), 0)),
                pl.BlockSpec((T, LANES), lambda b, c: (ci(b, c), P_SMALL // LANES)),
                pl.BlockSpec((N_SMALL, T), lambda b, c: (0, ci(b, c)))]

    return pl.pallas_call(
        _ssd_body,
        name="ssd_scan",
        grid=(bsz, ncc + nlc),
        in_specs=specs(cf) + specs(cr) + [small((1, 16)), small((16, 1)), small((1, 16)), small((16, 1))],
        out_specs=[pl.BlockSpec((T, W), lambda b, c: (cf(b, c), 0)), pl.BlockSpec((T, W), lambda b, c: (cr(b, c), 0))],
        out_shape=[jax.ShapeDtypeStruct((rows, W), F32)] * 2,
        scratch_shapes=[pltpu.VMEM((2 * SSD_HEADS, SSD_STATE, SSD_HD), F32)],
        compiler_params=_cparams("parallel", "arbitrary"),
    )(xbc, p, small_t, xbc, p, small_t, dt_bias.reshape(1, 16), dt_bias.reshape(16, 1), a.reshape(1, 16),
      a.reshape(16, 1))


def _ssd_fin_body(yf_ref, yb_ref, x_ref, z_ref, dsk_ref, g_ref, o_ref):
    y = yf_ref[...] + yb_ref[...] + dsk_ref[...] * x_ref[...]
    y = y * _silu(z_ref[...])
    o_ref[...] = y * lax.rsqrt(jnp.mean(y * y, axis=-1, keepdims=True) + EPS) * g_ref[...]


def _ssd_finish(yf, yb, xbc, p, d_skip, norm_g, rows):
    tm = _tile(rows, 512)
    blk = lambda col: pl.BlockSpec((tm, W), lambda i: (i, col))
    vec = pl.BlockSpec((1, W), lambda i: (0, 0))
    return pl.pallas_call(
        _ssd_fin_body,
        name="ssd_finish",
        grid=(rows // tm,),
        in_specs=[blk(0), blk(0), blk(0), blk(P_SSDZ // W), vec, vec],
        out_specs=blk(0),
        out_shape=jax.ShapeDtypeStruct((rows, W), F32),
        compiler_params=_cparams("parallel"),
    )(yf, yb, xbc, p, jnp.repeat(d_skip, SSD_HD).reshape(1, W), norm_g)


def _s5_disc_body(are_ref, aim_ref, ldt_ref, bre_ref, bim_ref, abre_ref, abim_ref, bbre_ref, bbim_ref):
    lam_re = jnp.minimum(are_ref[...], -1e-4)
    a_im = aim_ref[...]
    dt = jnp.exp(ldt_ref[...])
    mag = jnp.exp(lam_re * dt)
    ab_re, ab_im = mag * jnp.cos(a_im * dt), mag * jnp.sin(a_im * dt)
    den = lam_re * lam_re + a_im * a_im
    nr, ni = ab_re - 1.0, ab_im
    f_re = (nr * lam_re + ni * a_im) / den
    f_im = (ni * lam_re - nr * a_im) / den
    abre_ref[...] = ab_re
    abim_ref[...] = ab_im
    bbre_ref[...] = f_re * bre_ref[...] - f_im * bim_ref[...]
    bbim_ref[...] = f_re * bim_ref[...] + f_im * bre_ref[...]


def _s5_discretise(a_re, a_im, log_dt, b_re, b_im):
    gn = S5_NS
    col = lambda v: v.reshape(gn, 1)
    ldt = jnp.repeat(log_dt, S5_STATE).reshape(gn, 1)
    mat = lambda v: v.reshape(gn, S5_GROUP)
    cs = pl.BlockSpec((gn, 1), lambda: (0, 0))
    ms = pl.BlockSpec((gn, S5_GROUP), lambda: (0, 0))
    return pl.pallas_call(
        _s5_disc_body,
        name="s5_discretise",
        in_specs=[cs, cs, cs, ms, ms],
        out_specs=[cs, cs, ms, ms],
        out_shape=[jax.ShapeDtypeStruct((gn, 1), F32)] * 2 + [jax.ShapeDtypeStruct((gn, S5_GROUP), F32)] * 2,
    )(col(a_re), col(a_im), ldt, mat(b_re), mat(b_im))


def _s5_body(*refs, nchain, bsz):
    u_refs = refs[:nchain]
    pm_ref, pmt_ref, wb_ref, a_ref, wc_ref, y_ref, x_scr, st_scr = refs[nchain:]
    T = S5_T
    sw = S5_NS // S5_SUPER
    cw = W // S5_SUPER

    @pl.when(pl.program_id(0) == 0)
    def _():
        st_scr[...] = jnp.zeros_like(st_scr)

    stack = jnp.concatenate([r[...] for r in u_refs], axis=0).astype(BF16)
    lhs = _dot(pm_ref[...], stack)
    chain = lax.broadcasted_iota(I32, lhs.shape, 0) % nchain
    lhs_f = jnp.where(chain < bsz, lhs, 0.0).astype(BF16)
    lhs_b = jnp.where(chain < bsz, 0.0, lhs).astype(BF16)
    ys = []
    for g in range(S5_SUPER):
        cs = slice(g * cw, (g + 1) * cw)
        re = slice(g * sw, (g + 1) * sw)
        im = slice(S5_NS + g * sw, S5_NS + (g + 1) * sw)
        bu = _dot(jnp.concatenate([lhs_f[:, cs], lhs_b[:, cs]], axis=1), wb_ref[g])
        x_scr[:, re] = bu[:, :sw]
        x_scr[:, im] = bu[:, sw:]
        ar, ai = a_ref[:, re], a_ref[:, im]

        def step(j, carry):
            sr, si = carry
            r0 = pl.multiple_of(j * nchain, nchain)
            nr = ar * sr - ai * si + x_scr[pl.ds(r0, nchain), re]
            ni = ar * si + ai * sr + x_scr[pl.ds(r0, nchain), im]
            x_scr[pl.ds(r0, nchain), re] = nr
            x_scr[pl.ds(r0, nchain), im] = ni
            return nr, ni

        sr, si = lax.fori_loop(0, T, step, (st_scr[:, re], st_scr[:, im]))
        st_scr[:, re] = sr
        st_scr[:, im] = si
        wc = wc_ref[g]
        ys.append(_dot(x_scr[:, re].astype(BF16), wc[:sw]) + _dot(x_scr[:, im].astype(BF16), wc[sw:]))
    y = _dot_sel(pmt_ref[...], jnp.concatenate(ys, axis=1))
    for c in range(nchain):
        y_ref[c // bsz, c % bsz] = y[c * T:(c + 1) * T]


def _s5_scan(p, pm, pmt, wb, atab, wc, bsz, seq, ctx_len):
    T = S5_T
    nchain = 2 * bsz
    ncc, nlc = ctx_len // T, seq // T
    nsteps = ncc + nlc
    rows = nchain * T
    const = lambda shape: pl.BlockSpec(shape, lambda i: (0,) * len(shape))
    u_specs = []
    for c in range(nchain):
        rev = c >= bsz
        u_specs.append(pl.BlockSpec(
            (T, W), functools.partial(lambda i, b, rev: (_chunk_index(b, i, rev, ncc, nlc, bsz), P_S5 // W),
                                      b=c % bsz, rev=rev)))
    return pl.pallas_call(
        functools.partial(_s5_body, nchain=nchain, bsz=bsz),
        name="s5_scan",
        grid=(nsteps,),
        in_specs=u_specs + [const((rows, rows)), const((rows, rows)), const((S5_SUPER, 2 * W // S5_SUPER, 2 * S5_NS // S5_SUPER)),
                            const((nchain, 2 * S5_NS)), const((S5_SUPER, 2 * S5_NS // S5_SUPER, W // S5_SUPER))],
        out_specs=pl.BlockSpec((2, bsz, None, T, W), lambda i: (0, 0, i, 0, 0)),
        out_shape=jax.ShapeDtypeStruct((2, bsz, nsteps, T, W), F32),
        scratch_shapes=[pltpu.VMEM((rows, 2 * S5_NS), F32), pltpu.VMEM((nchain, 2 * S5_NS), F32)],
        compiler_params=_cparams("arbitrary"),
    )(*([p] * nchain), pm, pmt, wb, atab, wc)


def _s5_fin_body(yf_ref, yb_ref, u_ref, d_ref, w_ref, b_ref, o_ref, *, nchunk):
    yf = yf_ref[...].reshape(nchunk * S5_T, W)
    yb = jnp.concatenate([yb_ref[nchunk - 1 - q] for q in range(nchunk)], axis=0)
    y = yf + yb + d_ref[...] * u_ref[...]
    g = _gelu_tanh(y)
    o_ref[...] = g * _sigmoid(_dot(g.astype(BF16), w_ref[...].astype(BF16)) + b_ref[...])


def _s5_finish(ys, p, d_skip, glu_w, glu_b, rows, bsz, seq, ctx_len):
    tile = ctx_len
    g = tile // S5_T
    nlt = seq // tile
    n_lat = bsz * nlt
    samp = lambda i: jnp.where(i < n_lat, i // nlt, i - n_lat)
    fblk = lambda i: jnp.where(i < n_lat, 1 + i % nlt, 0)
    bblk = lambda i: jnp.where(i < n_lat, nlt - i % nlt, 0)
    vec = pl.BlockSpec((1, W), lambda i: (0, 0))
    return pl.pallas_call(
        functools.partial(_s5_fin_body, nchunk=g),
        name="s5_finish",
        grid=(rows // tile,),
        in_specs=[pl.BlockSpec((None, None, g, S5_T, W), lambda i: (0, samp(i), fblk(i), 0, 0)),
                  pl.BlockSpec((None, None, g, S5_T, W), lambda i: (1, samp(i), bblk(i), 0, 0)),
                  pl.BlockSpec((tile, W), lambda i: (i, P_S5 // W)), vec,
                  pl.BlockSpec((W, W), lambda i: (0, 0)), vec],
        out_specs=pl.BlockSpec((tile, W), lambda i: (i, 0)),
        out_shape=jax.ShapeDtypeStruct((rows, W), F32),
        compiler_params=_cparams("parallel"),
    )(ys, ys, p, d_skip.reshape(1, W), glu_w, glu_b.reshape(1, W))


def _block_diag(m):
    g, a, b = m.shape
    eye = jnp.eye(g, dtype=m.dtype)
    return (eye[:, None, :, None] * m[:, :, None, :]).reshape(g * a, g * b)


def _s5_branch(p, rows, bsz, seq, ctx_len, a_re, a_im, log_dt, b_re, b_im, c_re, c_im, d_skip, glu_w, glu_b):
    disc = [_s5_discretise(a_re[d], a_im[d], log_dt[d], b_re, b_im) for d in range(2)]
    gs = S5_GROUPS // S5_SUPER
    def bmat(v):
        v = v.reshape(S5_SUPER, gs, S5_STATE, S5_GROUP).transpose(0, 1, 3, 2)
        return jnp.stack([_block_diag(v[s]) for s in range(S5_SUPER)])
    wb = jnp.concatenate([jnp.concatenate([bmat(disc[d][2]), bmat(disc[d][3])], axis=2) for d in range(2)],
                         axis=1).astype(BF16)
    arow = lambda d: jnp.concatenate([disc[d][0].reshape(1, S5_NS), disc[d][1].reshape(1, S5_NS)], axis=1)
    atab = jnp.concatenate([jnp.broadcast_to(arow(0), (bsz, 2 * S5_NS)),
                            jnp.broadcast_to(arow(1), (bsz, 2 * S5_NS))], axis=0)
    def cmat(v):
        v = v.reshape(S5_SUPER, gs, S5_GROUP, S5_STATE).transpose(0, 1, 3, 2)
        return jnp.stack([_block_diag(v[s]) for s in range(S5_SUPER)])
    wc = jnp.concatenate([cmat(c_re), -cmat(c_im)], axis=1).astype(BF16)
    T, nchain = S5_T, 2 * bsz
    r = jnp.arange(nchain * T, dtype=I32)
    j, c = r // nchain, r % nchain
    src = c * T + jnp.where(c < bsz, j, T - 1 - j)
    pm = (src[:, None] == r[None, :])
    ys = _s5_scan(p, pm.astype(BF16), pm.T.astype(BF16), wb, atab, wc, bsz, seq, ctx_len)
    return _s5_finish(ys, p, d_skip, glu_w, glu_b, rows, bsz, seq, ctx_len)


def _merge_body(y0l, y0c, y1, y2, y3, g0, g1, g2, g3, w0, w1, w2, w3, o_ref, *, n_lat_tiles):
    hy = jnp.where(pl.program_id(0) < n_lat_tiles, y0l[...], y0c[...])
    acc = _sigmoid(g0[...]) * _dot(hy.astype(BF16), w0[...])
    for y, g, w in ((y1, g1, w1), (y2, g2, w2), (y3, g3, w3)):
        acc = acc + _sigmoid(g[...]) * _dot(y[...].astype(BF16), w[...])
    o_ref[...] = acc.astype(BF16)


def _merge(hy_lat, hy_ctx, ys, p, wb, rows):
    nl = hy_lat.shape[0]
    tm, tn = _tile(math.gcd(nl, hy_ctx.shape[0]), 512), 512
    n_lat = nl // tm
    ysp = pl.BlockSpec((tm, W), lambda i, j: (i, 0))
    gsp = lambda k: pl.BlockSpec((tm, tn), lambda i, j: (i, (P_GATE + k * D_MODEL) // tn + j))
    wsp = lambda k: pl.BlockSpec((None, W, tn), lambda i, j: (k, 0, j))
    return pl.pallas_call(
        functools.partial(_merge_body, n_lat_tiles=n_lat),
        name="merge",
        grid=(rows // tm, D_MODEL // tn),
        in_specs=[pl.BlockSpec((tm, W), lambda i, j: (jnp.minimum(i, n_lat - 1), 0)),
                  pl.BlockSpec((tm, W), lambda i, j: (jnp.maximum(i - n_lat, 0), 0))]
                 + [ysp] * 3 + [gsp(k) for k in range(4)] + [wsp(k) for k in range(4)],
        out_specs=pl.BlockSpec((tm, tn), lambda i, j: (i, j)),
        out_shape=jax.ShapeDtypeStruct((rows, D_MODEL), BF16),
        compiler_params=_cparams("parallel", "parallel"),
    )(hy_lat, hy_ctx, *ys, p, p, p, p, wb, wb, wb, wb)


def _outproj_body(a_ref, w_ref, xl_ref, xc_ref, mod_ref, o_ref, *, n_lat_tiles):
    y = mod_ref[2:3, :] * _dot(a_ref[...], w_ref[...])
    i = pl.program_id(0)

    @pl.when(i < n_lat_tiles)
    def _():
        o_ref[...] = xl_ref[...] + y

    @pl.when(i >= n_lat_tiles)
    def _():
        o_ref[...] = xc_ref[...] + y


def _outproj(acc, w, xl, xc, mods, seq, bsz):
    rows = acc.shape[0]
    nl, nc = xl.shape[0], xc.shape[0]
    tm = _tile(math.gcd(nl, nc), 512)
    tn = 512
    n_lat = nl // tm
    return pl.pallas_call(
        functools.partial(_outproj_body, n_lat_tiles=n_lat),
        name="outproj",
        grid=(rows // tm, D_MODEL // tn),
        in_specs=[
            pl.BlockSpec((tm, D_MODEL), lambda i, j: (i, 0)),
            pl.BlockSpec((D_MODEL, tn), lambda i, j: (0, j)),
            pl.BlockSpec((tm, tn), lambda i, j: (jnp.minimum(i, n_lat - 1), j)),
            pl.BlockSpec((tm, tn), lambda i, j: (jnp.maximum(i - n_lat, 0), j)),
            pl.BlockSpec((None, 6, tn), lambda i, j: (_mod_index(i, tm, n_lat, seq, bsz), 0, j)),
        ],
        out_specs=pl.BlockSpec((tm, tn), lambda i, j: (i, j)),
        out_shape=jax.ShapeDtypeStruct((rows, D_MODEL), F32),
        compiler_params=_cparams("parallel", "parallel"),
    )(acc, w, xl, xc, mods)


def _router_body(x_ref, g_ref, mod_ref, wr_ref, h_ref, p_ref):
    h = _norm_mod(x_ref[...], g_ref[...], mod_ref[3:4, :], mod_ref[4:5, :])
    h_ref[...] = h.astype(BF16)
    logits = _dot3(h, wr_ref[...])
    lane = lax.broadcasted_iota(I32, logits.shape, 1)
    logits = jnp.where(lane < N_EXPERTS, logits, -jnp.inf)
    e = jnp.exp(logits - jnp.max(logits, axis=-1, keepdims=True))
    p_ref[...] = e / jnp.sum(e, axis=-1, keepdims=True)


def _router(xm, g, mods, w_router, n_lat_rows, seq, bsz):
    rows = xm.shape[0]
    tm = _tile(math.gcd(n_lat_rows, seq), 512)
    n_lat = n_lat_rows // tm
    wr = jnp.pad(w_router, ((0, 0), (0, LANES - N_EXPERTS)))
    return pl.pallas_call(
        _router_body,
        name="moe_router",
        grid=(rows // tm,),
        in_specs=[
            pl.BlockSpec((tm, D_MODEL), lambda i: (i, 0)),
            pl.BlockSpec((1, D_MODEL), lambda i: (0, 0)),
            pl.BlockSpec((None, 6, D_MODEL), lambda i: (_mod_index(i, tm, n_lat, seq, bsz), 0, 0)),
            pl.BlockSpec((D_MODEL, LANES), lambda i: (0, 0)),
        ],
        out_specs=[pl.BlockSpec((tm, D_MODEL), lambda i: (i, 0)), pl.BlockSpec((tm, LANES), lambda i: (i, 0))],
        out_shape=[jax.ShapeDtypeStruct((rows, D_MODEL), BF16), jax.ShapeDtypeStruct((rows, LANES), F32)],
        compiler_params=_cparams("parallel"),
    )(xm, g, mods, wr)


def _route_geometry(n):
    cap = EC_CAPACITY * n // N_EXPERTS
    blk = min(n, ROUTE_BLOCK)
    nblk = n // blk
    nb1 = -(-(nblk + 1) // 8) * 8
    st = min(cap, LANES)
    return cap, blk, nblk, nb1, st


def _topk_body(p_ref, o_ref, cnt_ref, *, n, cap, blk):
    bits = pltpu.bitcast(p_ref[...], I32)
    capf = float(cap)

    def count(mask):
        return jnp.sum(jnp.where(mask, 1.0, 0.0), axis=0, keepdims=True)

    def vstep(i, thr):
        cand = thr | lax.shift_left(jnp.int32(1), 30 - i)
        return jnp.where(count(bits >= cand) >= capf, cand, thr)

    thr = lax.fori_loop(0, 31, vstep, jnp.zeros((1, LANES), I32))
    gt = bits > thr
    eq = bits == thr
    need = capf - count(gt)
    t = lax.broadcasted_iota(I32, bits.shape, 0)
    nbits = max(1, (n - 1).bit_length())

    def istep(i, j):
        cand = j + lax.shift_left(jnp.int32(1), nbits - 1 - i)
        return jnp.where(count(eq & (t < cand)) < need, cand, j)

    jmax = lax.fori_loop(0, nbits, istep, jnp.zeros((1, LANES), I32))
    sel = gt | (eq & (t <= jmax))
    self32 = jnp.where(sel, 1.0, 0.0)
    r = lax.broadcasted_iota(I32, (blk, blk), 0)
    c = lax.broadcasted_iota(I32, (blk, blk), 1)
    lower = jnp.where(c < r, 1.0, 0.0).astype(BF16)
    carry = jnp.zeros((1, LANES), F32)
    cnt_ref[...] = jnp.zeros_like(cnt_ref)
    for i in range(n // blk):
        sb = self32[i * blk:(i + 1) * blk]
        rank = _dot(lower, sb.astype(BF16)) + carry
        o_ref[i * blk:(i + 1) * blk, :] = jnp.where(sb > 0.0, rank, -1.0).astype(I32)
        carry = carry + jnp.sum(sb, axis=0, keepdims=True)
        cnt_ref[i + 1:i + 2, :] = carry.astype(I32)


def _topk(probs, row0, n, nsets, name):
    cap, blk, nblk, nb1, _ = _route_geometry(n)
    return pl.pallas_call(
        functools.partial(_topk_body, n=n, cap=cap, blk=blk),
        name=name,
        grid=(nsets,),
        in_specs=[pl.BlockSpec((n, LANES), lambda s: (row0 // n + s, 0))],
        out_specs=[pl.BlockSpec((n, LANES), lambda s: (s, 0)), pl.BlockSpec((None, nb1, LANES), lambda s: (s, 0, 0))],
        out_shape=[jax.ShapeDtypeStruct((nsets * n, LANES), I32), jax.ShapeDtypeStruct((nsets, nb1, LANES), I32)],
        compiler_params=_cparams("parallel"),
    )(probs)


def _gather_body(cnt_ref, slot_ref, p_ref, h_ref, xs_ref, gate_ref, acc, gacc, *, cap, blk, nblk, nb1, st):
    s = pl.program_id(0)
    e = pl.program_id(2)
    bounds = [cnt_ref[(s * nb1 + j) * N_EXPERTS + e] for j in range(nblk + 1)]
    lane = lax.broadcasted_iota(I32, (st, LANES), 1)
    for q in range(cap // st):
        lo, hi = q * st, (q + 1) * st
        b_lo = sum((bounds[j + 1] <= lo).astype(I32) for j in range(nblk))
        b_hi = sum((bounds[j] < hi).astype(I32) for j in range(nblk))
        rr = lax.broadcasted_iota(I32, (st, blk), 0) + lo
        acc[...] = jnp.zeros_like(acc)
        gacc[...] = jnp.zeros_like(gacc)

        def body(j, _):
            r0 = pl.multiple_of(j * blk, blk)
            onehot = jnp.where(slot_ref[e, pl.ds(j, 1), :] == rr, 1.0, 0.0).astype(BF16)
            acc[...] += _dot(onehot, h_ref[pl.ds(r0, blk), :])
            gacc[...] += _dot_sel(onehot, p_ref[pl.ds(r0, blk), :])
            return 0

        lax.fori_loop(b_lo, b_hi, body, 0)
        xs_ref[lo:hi, :] = acc[...].astype(BF16)
        gate_ref[lo:hi, :] = jnp.sum(jnp.where(lane == e, gacc[...], 0.0), axis=1, keepdims=True)


def _gather(slot_col, counts, probs, h2, row0, n, nsets, name):
    cap, blk, nblk, nb1, st = _route_geometry(n)
    dh = D_MODEL // 2
    slot_row = slot_col.reshape(nsets, nblk, blk, LANES)[..., :N_EXPERTS].transpose(0, 3, 1, 2)
    cnt = counts[:, :, :N_EXPERTS].reshape(-1)
    grid_spec = pltpu.PrefetchScalarGridSpec(
        num_scalar_prefetch=1,
        grid=(nsets, 2, N_EXPERTS),
        in_specs=[
            pl.BlockSpec((None, N_EXPERTS, nblk, blk), lambda s, k, e, c: (s, 0, 0, 0)),
            pl.BlockSpec((n, LANES), lambda s, k, e, c: (row0 // n + s, 0)),
            pl.BlockSpec((n, dh), lambda s, k, e, c: (row0 // n + s, k)),
        ],
        out_specs=[pl.BlockSpec((None, cap, dh), lambda s, k, e, c: (e, s, k)),
                   pl.BlockSpec((None, None, cap, 1), lambda s, k, e, c: (k, e, s, 0))],
        scratch_shapes=[pltpu.VMEM((st, dh), F32), pltpu.VMEM((st, LANES), F32)],
    )
    xs, gate = pl.pallas_call(
        functools.partial(_gather_body, cap=cap, blk=blk, nblk=nblk, nb1=nb1, st=st),
        name=name,
        grid_spec=grid_spec,
        out_shape=[jax.ShapeDtypeStruct((N_EXPERTS, nsets * cap, D_MODEL), BF16),
                   jax.ShapeDtypeStruct((2, N_EXPERTS, nsets * cap, 1), F32)],
        compiler_params=_cparams("parallel", "arbitrary", "arbitrary"),
    )(cnt, slot_row, probs, h2)
    return xs, gate[0]


def _ffn_body(*refs, with_ctx, ml):
    if with_ctx:
        xl_ref, gl_ref, xc_ref, gc_ref, wg_ref, wu_ref, wd_ref, yl_ref, yc_ref, acc = refs
        xs = jnp.concatenate([xl_ref[...], xc_ref[...]], axis=0)
    else:
        xl_ref, gl_ref, wg_ref, wu_ref, wd_ref, yl_ref, acc = refs
        xs = xl_ref[...]
    f = pl.program_id(2)

    @pl.when(f == 0)
    def _():
        acc[...] = jnp.zeros_like(acc)

    hid = _silu(_dot(xs, wg_ref[...].astype(BF16))) * _dot(xs, wu_ref[...].astype(BF16))
    acc[...] += _dot(hid.astype(BF16), wd_ref[...].astype(BF16))

    @pl.when(f == pl.num_programs(2) - 1)
    def _():
        yl_ref[...] = (acc[0:ml, :] * gl_ref[...]).astype(BF16)
        if with_ctx:
            yc_ref[...] = (acc[ml:, :] * gc_ref[...]).astype(BF16)


def _ffn(xs_l, g_l, xs_c, g_c, w_gate, w_up, w_down, layer):
    with_ctx = xs_c is not None
    nsplit = 2
    ml = xs_l.shape[1] // nsplit
    mc = xs_c.shape[1] // nsplit if with_ctx else 0
    fc = 256
    row = lambda m, last: pl.BlockSpec((None, m, last), lambda e, s, f: (e, s, 0))
    in_specs = [row(ml, D_MODEL), row(ml, 1)]
    args = [xs_l, g_l]
    out_specs = [row(ml, D_MODEL)]
    out_shape = [jax.ShapeDtypeStruct(xs_l.shape, BF16)]
    if with_ctx:
        in_specs += [row(mc, D_MODEL), row(mc, 1)]
        args += [xs_c, g_c]
        out_specs.append(row(mc, D_MODEL))
        out_shape.append(jax.ShapeDtypeStruct(xs_c.shape, BF16))
    in_specs += [pl.BlockSpec((None, None, D_MODEL, fc), lambda e, s, f: (layer, e, 0, f)),
                 pl.BlockSpec((None, None, D_MODEL, fc), lambda e, s, f: (layer, e, 0, f)),
                 pl.BlockSpec((None, None, fc, D_MODEL), lambda e, s, f: (layer, e, f, 0))]
    args += [w_gate, w_up, w_down]
    out = pl.pallas_call(
        functools.partial(_ffn_body, with_ctx=with_ctx, ml=ml),
        name="moe_ffn",
        grid=(N_EXPERTS, nsplit, D_EXPERT // fc),
        in_specs=in_specs,
        out_specs=out_specs,
        out_shape=out_shape,
        scratch_shapes=[pltpu.VMEM((ml + mc, D_MODEL), F32)],
        compiler_params=_cparams("parallel", "parallel", "arbitrary"),
    )(*args)
    return (out[0], out[1]) if with_ctx else (out[0], None)


def _scatter_body(cnt_ref, slot_ref, y_ref, x_ref, mod_ref, o_ref, acc, *, st, nb1, bpt):
    s, t, e = pl.program_id(0), pl.program_id(1), pl.program_id(2)

    @pl.when(e == 0)
    def _():
        acc[...] = jnp.zeros_like(acc)

    lo = cnt_ref[(s * nb1 + t * bpt) * N_EXPERTS + e]
    hi = cnt_ref[(s * nb1 + (t + 1) * bpt) * N_EXPERTS + e]
    shift = st.bit_length() - 1
    k_lo = lax.shift_right_logical(lo, shift)
    k_hi = lax.shift_right_logical(hi + (st - 1), shift)
    slot = slot_ref[...]
    lane = lax.broadcasted_iota(I32, slot.shape, 1)
    col = jnp.sum(jnp.where(lane == e, slot, 0).astype(F32), axis=1, keepdims=True)
    r = lax.broadcasted_iota(I32, (slot.shape[0], st), 1).astype(F32)

    def body(kb, _):
        k0 = pl.multiple_of(kb * st, st)
        onehot = jnp.where(col == r + k0.astype(F32), 1.0, 0.0).astype(BF16)
        acc[...] += _dot(onehot, y_ref[pl.ds(k0, st), :])
        return 0

    lax.fori_loop(k_lo, k_hi, body, 0)

    @pl.when(e == pl.num_programs(2) - 1)
    def _():
        o_ref[...] = x_ref[...] + mod_ref[5:6, :] * acc[...]


def _scatter(slot_col, counts, y, xm, mods, row0, n, nsets, mod_of_set, name):
    cap, blk, nblk, nb1, st = _route_geometry(n)
    tt = _tile(n, 1024)
    nt = n // tt
    cnt = counts[:, :, :N_EXPERTS].reshape(-1)
    grid_spec = pltpu.PrefetchScalarGridSpec(
        num_scalar_prefetch=1,
        grid=(nsets, nt, N_EXPERTS),
        in_specs=[
            pl.BlockSpec((tt, LANES), lambda s, t, e, c: (s * nt + t, 0)),
            pl.BlockSpec((None, cap, D_MODEL), lambda s, t, e, c: (e, s, 0)),
            pl.BlockSpec((tt, D_MODEL), lambda s, t, e, c: (row0 // tt + s * nt + t, 0)),
            pl.BlockSpec((None, 6, D_MODEL), lambda s, t, e, c: (mod_of_set(s), 0, 0)),
        ],
        out_specs=pl.BlockSpec((tt, D_MODEL), lambda s, t, e, c: (s * nt + t, 0)),
        scratch_shapes=[pltpu.VMEM((tt, D_MODEL), F32)],
    )
    return pl.pallas_call(
        functools.partial(_scatter_body, st=st, nb1=nb1, bpt=tt // blk),
        name=name,
        grid_spec=grid_spec,
        out_shape=jax.ShapeDtypeStruct((nsets * n, D_MODEL), F32),
        compiler_params=_cparams("parallel", "parallel", "arbitrary"),
    )(cnt, slot_col, y, xm, mods)


def _moe(xm, g, mods, w_router, w_gate, w_up, w_down, layer, n_lat_rows, bsz, seq, ctx_len, with_ctx):
    h2, probs = _router(xm, g, mods, w_router, n_lat_rows, seq, bsz)

    def route(row0, n, tag):
        slot_col, counts = _topk(probs, row0, n, bsz, "moe_topk" + tag)
        xs, gate = _gather(slot_col, counts, probs, h2, row0, n, bsz, "moe_gather" + tag)
        return slot_col, counts, xs, gate

    slot_l, cnt_l, xs_l, gate_l = route(0, seq, "_lat")
    if with_ctx:
        slot_c, cnt_c, xs_c, gate_c = route(n_lat_rows, ctx_len, "_ctx")
    else:
        xs_c = gate_c = None
    y_l, y_c = _ffn(xs_l, gate_l, xs_c, gate_c, w_gate, w_up, w_down, layer)
    xl = _scatter(slot_l, cnt_l, y_l, xm, mods, 0, seq, bsz, lambda s: s, "moe_scatter_lat")
    xc = None
    if with_ctx:
        xc = _scatter(slot_c, cnt_c, y_c, xm, mods, n_lat_rows, ctx_len, bsz, lambda s: bsz, "moe_scatter_ctx")
    return xl, xc


def _final_norm_body(x_ref, g_ref, o_ref):
    x = x_ref[...]
    o_ref[...] = x * lax.rsqrt(jnp.mean(x * x, axis=-1, keepdims=True) + EPS) * g_ref[...]


def _final_norm(x, g):
    rows, d = x.shape
    tm = _tile(rows, 512)
    return pl.pallas_call(
        _final_norm_body,
        name="final_norm",
        grid=(rows // tm,),
        in_specs=[pl.BlockSpec((tm, d), lambda i: (i, 0)), pl.BlockSpec((1, d), lambda i: (0, 0))],
        out_specs=pl.BlockSpec((tm, d), lambda i: (i, 0)),
        out_shape=jax.ShapeDtypeStruct((rows, d), F32),
        compiler_params=_cparams("parallel"),
    )(x, g)


def _reorder_w_in(w):
    pad = jnp.zeros((w.shape[0], P_WIDTH - P_SMALL - N_SMALL), w.dtype)
    return jnp.concatenate([w[:, 0:3584], w[:, 3600:5648], w[:, 5664:13856], w[:, 3584:3600], w[:, 5648:5664], pad],
                           axis=1).astype(BF16)


def _mixer(xl, xc, mods, lw, tabs_lat, tabs_ctx, bsz, seq, ctx_len, with_ctx_out):
    n_lat_rows = bsz * seq
    rows_all = n_lat_rows + bsz * ctx_len
    rows = rows_all if with_ctx_out else n_lat_rows
    h = _normmod(xl, xc, lw["norm_mix"], mods, seq, bsz)
    p = _matmul(h, lw["w_in"], 1024, 1024, "inproj")
    small_t = _small_t(p)
    ones = lambda n: jnp.ones((1, n), F32)
    u_hy = _short_conv(p, P_HY, 3 * W, lw["hy_conv_w"], lw["hy_conv_b"].reshape(1, -1), ones(3 * W), False,
                       n_lat_rows, ctx_len, "conv_hyena")
    hy_args = (lw["hy_w1"], lw["hy_b1"], lw["hy_fr1"], lw["hy_w2"], lw["hy_b2"], lw["hy_fr2"], lw["hy_w3"],
               lw["hy_decay"])

    def hy_filter(n, tabs, tag):
        gp, gm = _hy_filters(n, *hy_args, "hy_filter" + tag)
        return (_matmul(tabs[0], gp, 512, 512, "hy_spec_c" + tag), _matmul(tabs[1], gm, 512, 512, "hy_spec_s" + tag),
                lw["hy_bias"])

    hy_lat = _hyena(u_hy, 0, seq, bsz, tabs_lat, hy_filter(seq, tabs_lat, "_lat"), "_lat")
    hy_ctx = hy_lat
    if with_ctx_out:
        hy_ctx = _hyena(u_hy, n_lat_rows, ctx_len, bsz, tabs_ctx, hy_filter(ctx_len, tabs_ctx, "_ctx"), "_ctx")
    post = jnp.concatenate([jnp.ones((1, W), F32), jnp.full((1, W), ML_HD ** -0.5, F32)], axis=1)
    qk = _short_conv(p, P_MLQK, 2 * W, lw["ml_conv_w"], lw["ml_conv_b"].reshape(1, -1), post, True, n_lat_rows,
                     ctx_len, "conv_mlstm")
    hf, hb = _mlstm_scan(qk, p, small_t, lw["ml_gate_b"].reshape(16), bsz, seq, ctx_len)
    y_ml = _mlstm_finish(hf, hb, p, lw["ml_norm"].reshape(1, W), rows)
    y_s5 = _s5_branch(p, rows, bsz, seq, ctx_len, lw["s5_a_re"], lw["s5_a_im"], lw["s5_log_dt"], lw["s5_b_re"],
                      lw["s5_b_im"], lw["s5_c_re"], lw["s5_c_im"], lw["s5_d"], lw["s5_glu_w"], lw["s5_glu_b"])
    xbc = _short_conv(p, P_XBC, 2 * W, lw["ssd_conv_w"], lw["ssd_conv_b"].reshape(1, -1), ones(2 * W), True,
                      n_lat_rows, ctx_len, "conv_ssd")
    sf, sb = _ssd_scan(xbc, p, small_t, lw["ssd_dt_bias"].reshape(16), lw["ssd_a_log"].reshape(16), bsz, seq, ctx_len)
    y_ssd = _ssd_finish(sf, sb, xbc, p, lw["ssd_d"], lw["ssd_norm"].reshape(1, W), rows)
    acc = _merge(hy_lat, hy_ctx, (y_ml, y_s5, y_ssd), p, lw["w_branch"], rows)
    return _outproj(acc, lw["w_out"], xl, xc, mods, seq, bsz)


_PER_LAYER = ("ada_w", "ada_b", "norm_mix", "norm_ffn", "w_in", "hy_conv_w", "hy_conv_b", "hy_w1", "hy_b1", "hy_fr1",
              "hy_w2", "hy_b2", "hy_fr2", "hy_w3", "hy_decay", "hy_bias", "ml_conv_w", "ml_conv_b", "ml_gate_b",
              "ml_norm", "s5_a_re", "s5_a_im", "s5_log_dt", "s5_b_re", "s5_b_im", "s5_c_re", "s5_c_im", "s5_d",
              "s5_glu_w", "s5_glu_b", "ssd_conv_w", "ssd_conv_b", "ssd_dt_bias", "ssd_a_log", "ssd_d", "ssd_norm",
              "w_branch", "w_out", "w_router", "w_gate", "w_up", "w_down")


def kernel(x, c, ctx, c_ctx, ada_w, ada_b, norm_mix, norm_ffn, w_in, hy_conv_w, hy_conv_b, hy_w1, hy_b1, hy_fr1, hy_w2, hy_b2, hy_fr2, hy_w3, hy_decay, hy_bias, ml_conv_w, ml_conv_b, ml_gate_b, ml_norm, s5_a_re, s5_a_im, s5_log_dt, s5_b_re, s5_b_im, s5_c_re, s5_c_im, s5_d, s5_glu_w, s5_glu_b, ssd_conv_w, ssd_conv_b, ssd_dt_bias, ssd_a_log, ssd_d, ssd_norm, w_branch, w_out, w_router, w_gate, w_up, w_down, final_norm):
    stacked = dict(zip(_PER_LAYER, (ada_w, ada_b, norm_mix, norm_ffn, w_in, hy_conv_w, hy_conv_b, hy_w1, hy_b1, hy_fr1,
                                    hy_w2, hy_b2, hy_fr2, hy_w3, hy_decay, hy_bias, ml_conv_w, ml_conv_b, ml_gate_b,
                                    ml_norm, s5_a_re, s5_a_im, s5_log_dt, s5_b_re, s5_b_im, s5_c_re, s5_c_im, s5_d,
                                    s5_glu_w, s5_glu_b, ssd_conv_w, ssd_conv_b, ssd_dt_bias, ssd_a_log, ssd_d, ssd_norm,
                                    w_branch, w_out, w_router, w_gate, w_up, w_down)))
    bsz, seq, d = x.shape
    ctx_len = ctx.shape[1]
    depth = ada_w.shape[0]
    assert d == D_MODEL and 2 * bsz <= 8 and seq % ctx_len == 0 and ctx_len % SCAN_BLOCK == 0
    xl = x.reshape(bsz * seq, d)
    xc = ctx.reshape(bsz * ctx_len, d)
    cc = jnp.zeros((8, d), F32).at[:bsz].set(c).at[bsz].set(c_ctx)
    tabs_lat = _dft_tables(seq)
    tabs_ctx = _dft_tables(ctx_len)
    for i in range(depth):
        last = i == depth - 1
        lw = {k: v[i] for k, v in stacked.items() if k not in ("ada_w", "w_gate", "w_up", "w_down")}
        lw["w_in"] = _reorder_w_in(lw["w_in"])
        lw["norm_mix"] = lw["norm_mix"].reshape(1, d)
        lw["norm_ffn"] = lw["norm_ffn"].reshape(1, d)
        lw["w_branch"] = lw["w_branch"].astype(BF16)
        lw["w_out"] = lw["w_out"].astype(BF16)
        mods = _ada(cc, ada_w, lw["ada_b"].reshape(1, -1), i).reshape(8, 6, d)
        xm = _mixer(xl, xc, mods, lw, tabs_lat, tabs_ctx, bsz, seq, ctx_len, not last)
        xl, xc_new = _moe(xm, lw["norm_ffn"], mods, lw["w_router"], w_gate, w_up, w_down, i,
                          bsz * seq, bsz, seq, ctx_len, not last)
        if not last:
            xc = xc_new
    return _final_norm(xl, final_norm.reshape(1, d)).reshape(bsz, seq, d)
```

```python
import functools
import math

import jax
import jax.numpy as jnp
from jax import lax
from jax.experimental import pallas as pl
from jax.experimental.pallas import tpu as pltpu

F32 = jnp.float32
BF16 = jnp.bfloat16
I32 = jnp.int32

D_MODEL = 2048
W = 512
GRID_W = 64
EPS = 1e-6
HY_EMB = 33
HY_BANDS = 16
HY_FFN = 64
ML_HEADS = 4
ML_HD = 128
ML_CHUNK = 64
S5_GROUP = 16
S5_GROUPS = 32
S5_STATE = 64
S5_NS = S5_GROUPS * S5_STATE
S5_SUPER = 4
S5_T = 64
SSD_HD = 64
SSD_HEADS = 8
SSD_GROUPS = 2
SSD_STATE = 128
SSD_CHUNK = 128
N_EXPERTS = 16
EC_CAPACITY = 2
D_EXPERT = 1536
SCAN_BLOCK = 128
ROUTE_BLOCK = 256

P_HY = 0
P_MLQK = 1536
P_MLV = 2560
P_MLO = 3072
P_S5 = 3584
P_SSDZ = 4096
P_XBC = 4608
P_GATE = 5632
P_SMALL = 13824
P_WIDTH = 14336
N_SMALL = 32

LANES = 128
VMEM_LIMIT_BYTES = 56 * 1024 * 1024


def _cparams(*sem):
    return pltpu.CompilerParams(dimension_semantics=sem, vmem_limit_bytes=VMEM_LIMIT_BYTES)


def _dot(a, b):
    return jnp.dot(a, b, preferred_element_type=F32)


def _dot_nt(a, b):
    return lax.dot_general(a, b, (((1,), (1,)), ((), ())), preferred_element_type=F32)


def _split3(x):
    hi = x.astype(BF16)
    r = x - hi.astype(F32)
    mid = r.astype(BF16)
    lo = (r - mid.astype(F32)).astype(BF16)
    return hi, mid, lo


def _dot_sel(m01, x):
    hi, mid, lo = _split3(x)
    return _dot(m01, hi) + _dot(m01, mid) + _dot(m01, lo)


def _sel_dot(x, m01):
    hi, mid, lo = _split3(x)
    return _dot(hi, m01) + _dot(mid, m01) + _dot(lo, m01)


def _dot3(a, b):
    ah = a.astype(BF16)
    al = (a - ah.astype(F32)).astype(BF16)
    bh = b.astype(BF16)
    bl = (b - bh.astype(F32)).astype(BF16)
    return _dot(ah, bh) + _dot(al, bh) + _dot(ah, bl)


def _sigmoid(x):
    return 1.0 / (1.0 + jnp.exp(-x))


def _silu(x):
    return x * _sigmoid(x)


def _softplus(x):
    return jnp.maximum(x, 0.0) + jnp.log(1.0 + jnp.exp(-jnp.abs(x)))


def _log_sigmoid(x):
    return jnp.minimum(x, 0.0) - jnp.log(1.0 + jnp.exp(-jnp.abs(x)))


def _gelu_tanh(x):
    return 0.5 * x * (1.0 + jnp.tanh(math.sqrt(2.0 / math.pi) * (x + 0.044715 * (x * x * x))))


def _tile(n, pref):
    t = min(n, pref)
    while n % t:
        t //= 2
    return t


def _mod_index(i, tm, n_lat_tiles, seq, bsz):
    return jnp.where(i < n_lat_tiles, (i * tm) // seq, bsz)


def _mm_body(a_ref, b_ref, o_ref):
    o_ref[...] = _dot(a_ref[...].astype(BF16), b_ref[...].astype(BF16)).astype(o_ref.dtype)


def _matmul(a, b, tm, tn, name, out_dtype=F32):
    m, k = a.shape
    n = b.shape[1]
    tm, tn = _tile(m, tm), _tile(n, tn)
    return pl.pallas_call(
        _mm_body,
        name=name,
        grid=(m // tm, n // tn),
        in_specs=[pl.BlockSpec((tm, k), lambda i, j: (i, 0)), pl.BlockSpec((k, tn), lambda i, j: (0, j))],
        out_specs=pl.BlockSpec((tm, tn), lambda i, j: (i, j)),
        out_shape=jax.ShapeDtypeStruct((m, n), out_dtype),
        compiler_params=_cparams("parallel", "parallel"),
    )(a, b)


def _ada_body(c_ref, w_ref, b_ref, o_ref):
    c = c_ref[...]
    o_ref[...] = _dot3(_silu(c), w_ref[...]) + b_ref[...]


def _ada(cc, w, b, layer):
    _, d, n = w.shape
    tn = 1024
    return pl.pallas_call(
        _ada_body,
        name="ada_mod",
        grid=(n // tn,),
        in_specs=[pl.BlockSpec((8, d), lambda j: (0, 0)), pl.BlockSpec((None, d, tn), lambda j: (layer, 0, j)),
                  pl.BlockSpec((1, tn), lambda j: (0, j))],
        out_specs=pl.BlockSpec((8, tn), lambda j: (0, j)),
        out_shape=jax.ShapeDtypeStruct((8, n), F32),
        compiler_params=_cparams("parallel"),
    )(cc, w, b)


def _norm_mod(x, g, shift, scale):
    y = x * lax.rsqrt(jnp.mean(x * x, axis=-1, keepdims=True) + EPS) * g
    return y * (1.0 + scale) + shift


def _normmod_body(xl_ref, xc_ref, g_ref, mod_ref, h_ref, *, n_lat_tiles):
    i = pl.program_id(0)

    @pl.when(i < n_lat_tiles)
    def _():
        h_ref[...] = _norm_mod(xl_ref[...], g_ref[...], mod_ref[0:1, :], mod_ref[1:2, :]).astype(BF16)

    @pl.when(i >= n_lat_tiles)
    def _():
        h_ref[...] = _norm_mod(xc_ref[...], g_ref[...], mod_ref[0:1, :], mod_ref[1:2, :]).astype(BF16)


def _normmod(xl, xc, g, mods, seq, bsz):
    nl, nc = xl.shape[0], xc.shape[0]
    d = xl.shape[1]
    tm = _tile(math.gcd(nl, nc), 512)
    n_lat = nl // tm
    return pl.pallas_call(
        functools.partial(_normmod_body, n_lat_tiles=n_lat),
        name="mixer_normmod",
        grid=((nl + nc) // tm,),
        in_specs=[
            pl.BlockSpec((tm, d), lambda i: (jnp.minimum(i, n_lat - 1), 0)),
            pl.BlockSpec((tm, d), lambda i: (jnp.maximum(i - n_lat, 0), 0)),
            pl.BlockSpec((1, d), lambda i: (0, 0)),
            pl.BlockSpec((None, 6, d), lambda i: (_mod_index(i, tm, n_lat, seq, bsz), 0, 0)),
        ],
        out_specs=pl.BlockSpec((tm, d), lambda i: (i, 0)),
        out_shape=jax.ShapeDtypeStruct((nl + nc, d), BF16),
        compiler_params=_cparams("parallel"),
    )(xl, xc, g, mods)


def _small_t_body(p_ref, o_ref):
    o_ref[...] = p_ref[...].T[0:N_SMALL, :]


def _small_t(p):
    rows = p.shape[0]
    tm = _tile(rows, 512)
    return pl.pallas_call(
        _small_t_body,
        name="small_transpose",
        grid=(rows // tm,),
        in_specs=[pl.BlockSpec((tm, LANES), lambda i: (i, P_SMALL // LANES))],
        out_specs=pl.BlockSpec((N_SMALL, tm), lambda i: (0, i)),
        out_shape=jax.ShapeDtypeStruct((N_SMALL, rows), F32),
        compiler_params=_cparams("parallel"),
    )(p)


def _conv_body(u_ref, w_ref, b_ref, s_ref, o_ref, *, act, n_lat_tiles, tile, ctx_len):
    i = pl.program_id(0)
    u = u_ref[...]
    t = lax.broadcasted_iota(I32, u.shape, 0)
    pos = jnp.where(i < n_lat_tiles, t & (GRID_W - 1), lax.rem(t, ctx_len))
    last = jnp.where(i < n_lat_tiles, GRID_W - 1, ctx_len - 1)
    prev = jnp.where(pos == 0, 0.0, pltpu.roll(u, 1, 0))
    nxt = jnp.where(pos == last, 0.0, pltpu.roll(u, tile - 1, 0))
    y = w_ref[0:1, :] * prev + w_ref[1:2, :] * u + w_ref[2:3, :] * nxt + b_ref[...]
    if act:
        y = _silu(y)
    o_ref[...] = y * s_ref[...]


def _short_conv(p, col0, width, w, b, post, act, n_lat_rows, ctx_len, name):
    rows = p.shape[0]
    tile = _tile(math.gcd(n_lat_rows, rows - n_lat_rows), 1024)
    assert tile % ctx_len == 0 and tile % GRID_W == 0
    cb = 512
    return pl.pallas_call(
        functools.partial(_conv_body, act=act, n_lat_tiles=n_lat_rows // tile, tile=tile, ctx_len=ctx_len),
        name=name,
        grid=(rows // tile, width // cb),
        in_specs=[
            pl.BlockSpec((tile, cb), lambda i, j: (i, col0 // cb + j)),
            pl.BlockSpec((3, cb), lambda i, j: (0, j)),
            pl.BlockSpec((1, cb), lambda i, j: (0, j)),
            pl.BlockSpec((1, cb), lambda i, j: (0, j)),
        ],
        out_specs=pl.BlockSpec((tile, cb), lambda i, j: (i, j)),
        out_shape=jax.ShapeDtypeStruct((rows, width), F32),
        compiler_params=_cparams("parallel", "parallel"),
    )(p, w, b, post)


def _hy_filter_body(feat_ref, w1_ref, b1_ref, fr1_ref, w2_ref, b2_ref, fr2_ref, w3a_ref, w3b_ref,
                    da_ref, db_ref, gp_ref, gm_ref, *, seq):
    hdn = jnp.sin(fr1_ref[...] * (_dot3(feat_ref[...], w1_ref[...]) + b1_ref[...]))
    hdn = jnp.sin(fr2_ref[...] * (_dot3(hdn, w2_ref[...]) + b2_ref[...]))
    t = lax.broadcasted_iota(I32, (seq, 1), 0)
    tn = t.astype(F32) / float(seq - 1)
    hf = _dot3(hdn, w3a_ref[...]) * jnp.exp(-tn * jnp.abs(da_ref[...]))
    hb = _dot3(hdn, w3b_ref[...]) * jnp.exp(-tn * jnp.abs(db_ref[...]))
    norm = jnp.sum(jnp.abs(hf) + jnp.abs(hb), axis=0, keepdims=True)
    hf = hf / norm
    hb = jnp.where(t == 0, 0.0, hb / norm)
    gp_ref[...] = hf + hb
    gm_ref[...] = hf - hb


def _hy_filters(seq, w1, b1, fr1, w2, b2, fr2, w3, decay, name):
    t = jnp.arange(seq, dtype=F32)
    freqs = jnp.linspace(1e-4, HY_BANDS - 1, HY_BANDS, dtype=F32)
    ang = (2.0 * math.pi / seq) * t[:, None] * freqs[None, :]
    feats = jnp.concatenate([(t / (seq - 1))[:, None], jnp.cos(ang), -jnp.sin(ang)], axis=-1)
    hp = LANES
    feats = jnp.pad(feats, ((0, 0), (0, hp - HY_EMB)))
    w1p = jnp.pad(w1, ((0, hp - HY_EMB), (0, hp - HY_FFN)))
    w2p = jnp.pad(w2, ((0, hp - HY_FFN), (0, hp - HY_FFN)))
    w3p = jnp.pad(w3, ((0, hp - HY_FFN), (0, 0))).reshape(hp, 4, W).transpose(1, 0, 2)
    row = lambda v: jnp.pad(v, (0, hp - HY_FFN)).reshape(1, hp)
    dec = decay.reshape(4, 1, W)
    cb = 256
    full = lambda shape: pl.BlockSpec(shape, lambda o, j: (0,) * len(shape))
    gp, gm = pl.pallas_call(
        functools.partial(_hy_filter_body, seq=seq),
        name=name,
        grid=(2, W // cb),
        in_specs=[
            full((seq, hp)), full((hp, hp)), full((1, hp)), full((1, hp)),
            full((hp, hp)), full((1, hp)), full((1, hp)),
            pl.BlockSpec((None, hp, cb), lambda o, j: (2 * o, 0, j)),
            pl.BlockSpec((None, hp, cb), lambda o, j: (2 * o + 1, 0, j)),
            pl.BlockSpec((None, 1, cb), lambda o, j: (2 * o, 0, j)),
            pl.BlockSpec((None, 1, cb), lambda o, j: (2 * o + 1, 0, j)),
        ],
        out_specs=[pl.BlockSpec((seq, cb), lambda o, j: (0, o * (W // cb) + j))] * 2,
        out_shape=[jax.ShapeDtypeStruct((seq, 2 * W), F32)] * 2,
        compiler_params=_cparams("parallel", "parallel"),
    )(feats, w1p, row(b1), row(fr1), w2p, row(b2), row(fr2), w3p, w3p, dec, dec)
    return gp, gm


DFT_RADIX = 64


def _dft_body(ca_ref, sa_ref, cb_ref, sb_ref, ea_ref, eb_ref, c_ref, s_ref):
    ea, eb = ea_ref[...], eb_ref[...]
    ca, sa = _sel_dot(ca_ref[...], ea), _sel_dot(sa_ref[...], ea)
    cb, sb = _sel_dot(cb_ref[...], eb), _sel_dot(sb_ref[...], eb)
    c_ref[...] = (ca * cb - sa * sb).astype(BF16)
    s_ref[...] = (sa * cb + ca * sb).astype(BF16)


def _dft_tables(seq):
    rdx = DFT_RADIX
    r = jnp.arange(seq, dtype=I32)[:, None]
    a = jnp.arange(rdx, dtype=I32)[None, :]
    col = jnp.arange(seq, dtype=I32)[None, :]
    ea = jnp.where(col // rdx == a.T, 1.0, 0.0).astype(BF16)
    eb = jnp.where(col % rdx == a.T, 1.0, 0.0).astype(BF16)

    def small(m):
        ang = (m % (4 * seq)).astype(F32) * (math.pi / (2 * seq))
        return jnp.cos(ang), jnp.sin(ang)

    def build(ma, mb, name):
        (ca, sa), (cb, sb) = small(ma), small(mb)
        tr = _tile(seq, 256)
        sm = pl.BlockSpec((tr, rdx), lambda i: (i, 0))
        ex = pl.BlockSpec((rdx, seq), lambda i: (0, 0))
        return pl.pallas_call(
            _dft_body,
            name=name,
            grid=(seq // tr,),
            in_specs=[sm, sm, sm, sm, ex, ex],
            out_specs=[pl.BlockSpec((tr, seq), lambda i: (i, 0))] * 2,
            out_shape=[jax.ShapeDtypeStruct((seq, seq), BF16)] * 2,
            compiler_params=_cparams("parallel"),
        )(ca, sa, cb, sb, ea, eb)

    c, s = build((2 * r + 1) * (rdx * a), (2 * r + 1) * a, "dft_table")
    ct, st = build(r * (2 * rdx * a), r * (2 * a + 1), "dft_table_t")
    return c, s, ct, st


def _hy_fwd_body(c_ref, s_ref, z_ref, gc_ref, gs_ref, p1_ref, p2_ref):
    z = z_ref[...].astype(BF16)
    zc = _dot(c_ref[...], z)
    zs = _dot(s_ref[...], z)
    gc, gs = gc_ref[...], gs_ref[...]
    p1_ref[...] = (zc * gc - zs * gs).astype(BF16)
    p2_ref[...] = (zc * gs + zs * gc).astype(BF16)


def _hy_fwd(ctab, stab, z, zcol, zrow0, gc, gs, gcol, seq, nseq, name):
    tk = _tile(seq, 512)
    nk = seq // tk
    zb0 = zrow0 // seq
    return pl.pallas_call(
        _hy_fwd_body,
        name=name,
        grid=(nseq, nk),
        in_specs=[
            pl.BlockSpec((tk, seq), lambda b, k: (k, 0)),
            pl.BlockSpec((tk, seq), lambda b, k: (k, 0)),
            pl.BlockSpec((seq, W), lambda b, k: (zb0 + b, zcol)),
            pl.BlockSpec((tk, W), lambda b, k: (k, gcol)),
            pl.BlockSpec((tk, W), lambda b, k: (k, gcol)),
        ],
        out_specs=[pl.BlockSpec((tk, W), lambda b, k: (b * nk + k, 0))] * 2,
        out_shape=[jax.ShapeDtypeStruct((nseq * seq, W), BF16)] * 2,
        compiler_params=_cparams("parallel", "arbitrary"),
    )(ctab, stab, z, gc, gs)


def _hy_inv_body(ct_ref, st_ref, p1_ref, p2_ref, zin_ref, mul_ref, bias_ref, o_ref, *, seq):
    y = (_dot(ct_ref[...], p1_ref[...]) + _dot(st_ref[...], p2_ref[...])) * (1.0 / seq)
    o_ref[...] = mul_ref[...] * (y + bias_ref[...] * zin_ref[...])


def _hy_inv(cttab, sttab, p1, p2, zin, zin_col, zin_row0, mul, mul_col, mul_row0, bias, seq, nseq, name):
    tt = _tile(seq, 512)
    nt = seq // tt
    zr, mr = zin_row0 // tt, mul_row0 // tt
    return pl.pallas_call(
        functools.partial(_hy_inv_body, seq=seq),
        name=name,
        grid=(nseq, nt),
        in_specs=[
            pl.BlockSpec((tt, seq), lambda b, t: (t, 0)),
            pl.BlockSpec((tt, seq), lambda b, t: (t, 0)),
            pl.BlockSpec((seq, W), lambda b, t: (b, 0)),
            pl.BlockSpec((seq, W), lambda b, t: (b, 0)),
            pl.BlockSpec((tt, W), lambda b, t: (zr + b * nt + t, zin_col)),
            pl.BlockSpec((tt, W), lambda b, t: (mr + b * nt + t, mul_col)),
            pl.BlockSpec((1, W), lambda b, t: (0, 0)),
        ],
        out_specs=pl.BlockSpec((tt, W), lambda b, t: (b * nt + t, 0)),
        out_shape=jax.ShapeDtypeStruct((nseq * seq, W), F32),
        compiler_params=_cparams("parallel", "arbitrary"),
    )(cttab, sttab, p1, p2, zin, mul, bias)


def _hyena(u, row0, seq, nseq, tabs, filt, tag):
    ctab, stab, cttab, sttab = tabs
    gc, gs, bias = filt
    p1, p2 = _hy_fwd(ctab, stab, u, 0, row0, gc, gs, 0, seq, nseq, "hy_fwd1" + tag)
    z2 = _hy_inv(cttab, sttab, p1, p2, u, 0, row0, u, 1, row0, bias[0:1], seq, nseq, "hy_inv1" + tag)
    p1, p2 = _hy_fwd(ctab, stab, z2, 0, 0, gc, gs, 1, seq, nseq, "hy_fwd2" + tag)
    return _hy_inv(cttab, sttab, p1, p2, z2, 0, 0, u, 2, row0, bias[1:2], seq, nseq, "hy_inv2" + tag)


def _chunk_index(b, c, rev, n_ctx_chunks, n_lat_chunks, bsz):
    in_ctx = c < n_ctx_chunks
    if rev:
        cc = n_ctx_chunks - 1 - c
        lc = n_lat_chunks - 1 - (c - n_ctx_chunks)
    else:
        cc = c
        lc = c - n_ctx_chunks
    return jnp.where(in_ctx, bsz * n_lat_chunks + b * n_ctx_chunks + cc, b * n_lat_chunks + lc)


def _tri(n, rev):
    r = lax.broadcasted_iota(I32, (n, n), 0)
    c = lax.broadcasted_iota(I32, (n, n), 1)
    return (c >= r) if rev else (c <= r)


def _mlstm_chunk(q, k, v, ig_col, ig_row, b_col, b_row, ct, nrow, m, mask, last):
    a_col = b_col + m
    dmat = jnp.where(mask, b_col - b_row + ig_row, -jnp.inf)
    mt = jnp.maximum(a_col, jnp.max(dmat, axis=1, keepdims=True))
    inter = jnp.exp(a_col - mt)
    qb, kb = q.astype(BF16), k.astype(BF16)
    s = _dot_nt(qb, kb) * jnp.exp(dmat - mt)
    num = _dot(s.astype(BF16), v.astype(BF16)) + inter * _dot(qb, ct.astype(BF16))
    den = jnp.sum(s, axis=1, keepdims=True) + inter * jnp.sum(q * nrow, axis=1, keepdims=True)
    h = num / jnp.maximum(jnp.abs(den), jnp.exp(-mt))
    m_new = mt[last:last + 1, :]
    tot = b_col[last:last + 1, :]
    ws = jnp.exp(tot - b_col + ig_col - m_new)
    dec = jnp.exp(tot + m - m_new)
    ct_new = dec * ct + _dot(k.T.astype(BF16), (v * ws).astype(BF16))
    n_new = dec * nrow + jnp.sum(k * ws, axis=0, keepdims=True)
    return h, ct_new, n_new, m_new


def _mlstm_body(qkf_ref, vf_ref, gcf_ref, grf_ref, qkb_ref, vb_ref, gcb_ref, grb_ref, gbc_ref, gbr_ref,
                hf_ref, hb_ref, ct_scr, n_scr, m_scr):
    T = ML_CHUNK

    @pl.when(pl.program_id(1) == 0)
    def _():
        ct_scr[...] = jnp.zeros_like(ct_scr)
        n_scr[...] = jnp.zeros_like(n_scr)
        m_scr[...] = jnp.zeros_like(m_scr)

    dirs = ((qkf_ref, vf_ref, gcf_ref, grf_ref, hf_ref), (qkb_ref, vb_ref, gcb_ref, grb_ref, hb_ref))
    for d, (qk_ref, v_ref, gc_ref, gr_ref, h_ref) in enumerate(dirs):
        rev = d == 1
        mask = _tri(T, rev)
        m01 = jnp.where(mask, 1.0, 0.0).astype(BF16)
        m01t = jnp.where(_tri(T, not rev), 1.0, 0.0).astype(BF16)
        gcol = gc_ref[:, 0:16] + gbc_ref[...]
        grow = gr_ref[0:16, :] + gbr_ref[...]
        ls_col, ls_row = _log_sigmoid(gcol), _log_sigmoid(grow)
        last = 0 if rev else T - 1
        nsub = SCAN_BLOCK // T
        carry = [(ct_scr[d * ML_HEADS + h], n_scr[d * ML_HEADS + h:d * ML_HEADS + h + 1, :],
                  m_scr[d * ML_HEADS + h:d * ML_HEADS + h + 1, 0:1]) for h in range(ML_HEADS)]
        for sub in (range(nsub - 1, -1, -1) if rev else range(nsub)):
            rs = slice(sub * T, (sub + 1) * T)
            cum_col = _dot_sel(m01, ls_col[rs])
            cum_row = _sel_dot(ls_row[:, rs], m01t)
            for h in range(ML_HEADS):
                ci, cf = 2 * d * ML_HEADS + h, (2 * d + 1) * ML_HEADS + h
                hs = slice(h * ML_HD, (h + 1) * ML_HD)
                ks = slice(W + h * ML_HD, W + (h + 1) * ML_HD)
                ct, nrow, m = carry[h]
                hh, ct, nrow, m = _mlstm_chunk(
                    qk_ref[rs, hs], qk_ref[rs, ks], v_ref[rs, hs],
                    gcol[rs, ci:ci + 1], grow[ci:ci + 1, rs], cum_col[:, cf:cf + 1], cum_row[cf:cf + 1, :],
                    ct, nrow, m, mask, last)
                h_ref[rs, hs] = hh
                carry[h] = (ct, nrow, m)
        for h in range(ML_HEADS):
            ct, nrow, m = carry[h]
            j = d * ML_HEADS + h
            ct_scr[j] = ct
            n_scr[j:j + 1, :] = nrow
            m_scr[j:j + 1, :] = jnp.broadcast_to(m, (1, LANES))


def _mlstm_scan(qk, p, small_t, gate_b, bsz, seq, ctx_len):
    T = SCAN_BLOCK
    rows = qk.shape[0]
    ncc, nlc = ctx_len // T, seq // T
    cf = functools.partial(_chunk_index, rev=False, n_ctx_chunks=ncc, n_lat_chunks=nlc, bsz=bsz)
    cr = functools.partial(_chunk_index, rev=True, n_ctx_chunks=ncc, n_lat_chunks=nlc, bsz=bsz)

    def specs(ci):
        return [pl.BlockSpec((T, 2 * W), lambda b, c: (ci(b, c), 0)),
                pl.BlockSpec((T, W), lambda b, c: (ci(b, c), P_MLV // W)),
                pl.BlockSpec((T, LANES), lambda b, c: (ci(b, c), P_SMALL // LANES)),
                pl.BlockSpec((N_SMALL, T), lambda b, c: (0, ci(b, c)))]

    nchain = 2 * ML_HEADS
    return pl.pallas_call(
        _mlstm_body,
        name="mlstm_scan",
        grid=(bsz, ncc + nlc),
        in_specs=specs(cf) + specs(cr) + [pl.BlockSpec((1, 16), lambda b, c: (0, 0)),
                                         pl.BlockSpec((16, 1), lambda b, c: (0, 0))],
        out_specs=[pl.BlockSpec((T, W), lambda b, c: (cf(b, c), 0)), pl.BlockSpec((T, W), lambda b, c: (cr(b, c), 0))],
        out_shape=[jax.ShapeDtypeStruct((rows, W), F32)] * 2,
        scratch_shapes=[pltpu.VMEM((nchain, ML_HD, ML_HD), F32), pltpu.VMEM((nchain, LANES), F32),
                        pltpu.VMEM((nchain, LANES), F32)],
        compiler_params=_cparams("parallel", "arbitrary"),
    )(qk, p, p, small_t, qk, p, p, small_t, gate_b.reshape(1, 16), gate_b.reshape(16, 1))


def _mlstm_fin_body(hf_ref, hb_ref, o_ref, g_ref, y_ref):
    h = hf_ref[...] + hb_ref[...]
    for i in range(ML_HEADS):
        hh = h[:, i * ML_HD:(i + 1) * ML_HD]
        hh = hh * lax.rsqrt(jnp.mean(hh * hh, axis=-1, keepdims=True) + EPS)
        sl = slice(i * ML_HD, (i + 1) * ML_HD)
        y_ref[:, sl] = hh * g_ref[:, sl] * _sigmoid(o_ref[:, sl])


def _mlstm_finish(hf, hb, p, norm_g, rows):
    tm = _tile(rows, 512)
    return pl.pallas_call(
        _mlstm_fin_body,
        name="mlstm_finish",
        grid=(rows // tm,),
        in_specs=[pl.BlockSpec((tm, W), lambda i: (i, 0)), pl.BlockSpec((tm, W), lambda i: (i, 0)),
                  pl.BlockSpec((tm, W), lambda i: (i, P_MLO // W)), pl.BlockSpec((1, W), lambda i: (0, 0))],
        out_specs=pl.BlockSpec((tm, W), lambda i: (i, 0)),
        out_shape=jax.ShapeDtypeStruct((rows, W), F32),
        compiler_params=_cparams("parallel"),
    )(hf, hb, p, norm_g)


def _ssd_body(xf_ref, dcf_ref, drf_ref, xb_ref, dcb_ref, drb_ref, dbc_ref, dbr_ref, ac_ref, ar_ref,
              yf_ref, yb_ref, st_scr):
    T = SSD_CHUNK

    @pl.when(pl.program_id(1) == 0)
    def _():
        st_scr[...] = jnp.zeros_like(st_scr)

    for d, (xbc_ref, dcol_ref, drow_ref, y_ref) in enumerate(((xf_ref, dcf_ref, drf_ref, yf_ref),
                                                              (xb_ref, dcb_ref, drb_ref, yb_ref))):
        rev = d == 1
        mask = _tri(T, rev)
        m01 = jnp.where(mask, 1.0, 0.0).astype(BF16)
        m01t = jnp.where(_tri(T, not rev), 1.0, 0.0).astype(BF16)
        dt_col = _softplus(dcol_ref[:, 16:32] + dbc_ref[...])
        dt_row = _softplus(drow_ref[16:32, :] + dbr_ref[...])
        acs_col = _dot_sel(m01, dt_col * ac_ref[...])
        acs_row = _sel_dot(dt_row * ar_ref[...], m01t)
        last = 0 if rev else T - 1
        for g in range(SSD_GROUPS):
            bm = xbc_ref[:, W + g * SSD_STATE:W + (g + 1) * SSD_STATE]
            cm = xbc_ref[:, W + (SSD_GROUPS + g) * SSD_STATE:W + (SSD_GROUPS + g + 1) * SSD_STATE]
            bmb, cmb = bm.astype(BF16), cm.astype(BF16)
            cb = _dot_nt(cmb, bmb)
            bmt = bm.T.astype(BF16)
            for hh in range(SSD_HEADS // SSD_GROUPS):
                h = g * (SSD_HEADS // SSD_GROUPS) + hh
                ci = d * SSD_HEADS + h
                a_col, a_row = acs_col[:, ci:ci + 1], acs_row[ci:ci + 1, :]
                x = xbc_ref[:, h * SSD_HD:(h + 1) * SSD_HD] * dt_col[:, ci:ci + 1]
                lm = jnp.where(mask, jnp.exp(jnp.where(mask, a_col - a_row, 0.0)), 0.0)
                st = st_scr[ci]
                y = _dot((cb * lm).astype(BF16), x.astype(BF16)) + _dot(cmb, st.astype(BF16)) * jnp.exp(a_col)
                y_ref[:, h * SSD_HD:(h + 1) * SSD_HD] = y
                tot = a_col[last:last + 1, :]
                xd = x * jnp.exp(tot - a_col)
                st_scr[ci] = jnp.exp(tot) * st + _dot(bmt, xd.astype(BF16))


def _ssd_scan(xbc, p, small_t, dt_bias, a_log, bsz, seq, ctx_len):
    T = SSD_CHUNK
    rows = xbc.shape[0]
    ncc, nlc = ctx_len // T, seq // T
    cf = functools.partial(_chunk_index, rev=False, n_ctx_chunks=ncc, n_lat_chunks=nlc, bsz=bsz)
    cr = functools.partial(_chunk_index, rev=True, n_ctx_chunks=ncc, n_lat_chunks=nlc, bsz=bsz)
    a = -jnp.exp(a_log.astype(F32))
    small = lambda shape: pl.BlockSpec(shape, lambda b, c: (0, 0))

    def specs(ci):
        return [pl.BlockSpec((T, 2 * W), lambda b, c: (ci(b, c), 0)),
                pl.BlockSpec((T, LANES), lambda b, c: (ci(b, c), P_SMALL // LANES)),
                pl.BlockSpec((N_SMALL, T), lambda b, c: (0, ci(b, c)))]

    return pl.pallas_call(
        _ssd_body,
        name="ssd_scan",
        grid=(bsz, ncc + nlc),
        in_specs=specs(cf) + specs(cr) + [small((1, 16)), small((16, 1)), small((1, 16)), small((16, 1))],
        out_specs=[pl.BlockSpec((T, W), lambda b, c: (cf(b, c), 0)), pl.BlockSpec((T, W), lambda b, c: (cr(b, c), 0))],
        out_shape=[jax.ShapeDtypeStruct((rows, W), F32)] * 2,
        scratch_shapes=[pltpu.VMEM((2 * SSD_HEADS, SSD_STATE, SSD_HD), F32)],
        compiler_params=_cparams("parallel", "arbitrary"),
    )(xbc, p, small_t, xbc, p, small_t, dt_bias.reshape(1, 16), dt_bias.reshape(16, 1), a.reshape(1, 16),
      a.reshape(16, 1))


def _ssd_fin_body(yf_ref, yb_ref, x_ref, z_ref, dsk_ref, g_ref, o_ref):
    y = yf_ref[...] + yb_ref[...] + dsk_ref[...] * x_ref[...]
    y = y * _silu(z_ref[...])
    o_ref[...] = y * lax.rsqrt(jnp.mean(y * y, axis=-1, keepdims=True) + EPS) * g_ref[...]


def _ssd_finish(yf, yb, xbc, p, d_skip, norm_g, rows):
    tm = _tile(rows, 512)
    blk = lambda col: pl.BlockSpec((tm, W), lambda i: (i, col))
    vec = pl.BlockSpec((1, W), lambda i: (0, 0))
    return pl.pallas_call(
        _ssd_fin_body,
        name="ssd_finish",
        grid=(rows // tm,),
        in_specs=[blk(0), blk(0), blk(0), blk(P_SSDZ // W), vec, vec],
        out_specs=blk(0),
        out_shape=jax.ShapeDtypeStruct((rows, W), F32),
        compiler_params=_cparams("parallel"),
    )(yf, yb, xbc, p, jnp.repeat(d_skip, SSD_HD).reshape(1, W), norm_g)


def _s5_disc_body(are_ref, aim_ref, ldt_ref, bre_ref, bim_ref, abre_ref, abim_ref, bbre_ref, bbim_ref):
    lam_re = jnp.minimum(are_ref[...], -1e-4)
    a_im = aim_ref[...]
    dt = jnp.exp(ldt_ref[...])
    mag = jnp.exp(lam_re * dt)
    ab_re, ab_im = mag * jnp.cos(a_im * dt), mag * jnp.sin(a_im * dt)
    den = lam_re * lam_re + a_im * a_im
    nr, ni = ab_re - 1.0, ab_im
    f_re = (nr * lam_re + ni * a_im) / den
    f_im = (ni * lam_re - nr * a_im) / den
    abre_ref[...] = ab_re
    abim_ref[...] = ab_im
    bbre_ref[...] = f_re * bre_ref[...] - f_im * bim_ref[...]
    bbim_ref[...] = f_re * bim_ref[...] + f_im * bre_ref[...]


def _s5_discretise(a_re, a_im, log_dt, b_re, b_im):
    gn = S5_NS
    col = lambda v: v.reshape(gn, 1)
    ldt = jnp.repeat(log_dt, S5_STATE).reshape(gn, 1)
    mat = lambda v: v.reshape(gn, S5_GROUP)
    cs = pl.BlockSpec((gn, 1), lambda: (0, 0))
    ms = pl.BlockSpec((gn, S5_GROUP), lambda: (0, 0))
    return pl.pallas_call(
        _s5_disc_body,
        name="s5_discretise",
        in_specs=[cs, cs, cs, ms, ms],
        out_specs=[cs, cs, ms, ms],
        out_shape=[jax.ShapeDtypeStruct((gn, 1), F32)] * 2 + [jax.ShapeDtypeStruct((gn, S5_GROUP), F32)] * 2,
    )(col(a_re), col(a_im), ldt, mat(b_re), mat(b_im))


def _s5_body(*refs, nchain, bsz):
    u_refs = refs[:nchain]
    pm_ref, pmt_ref, wb_ref, a_ref, wc_ref, y_ref, x_scr, st_scr = refs[nchain:]
    T = S5_T
    sw = S5_NS // S5_SUPER
    cw = W // S5_SUPER

    @pl.when(pl.program_id(0) == 0)
    def _():
        st_scr[...] = jnp.zeros_like(st_scr)

    stack = jnp.concatenate([r[...] for r in u_refs], axis=0).astype(BF16)
    lhs = _dot(pm_ref[...], stack)
    chain = lax.broadcasted_iota(I32, lhs.shape, 0) % nchain
    lhs_f = jnp.where(chain < bsz, lhs, 0.0).astype(BF16)
    lhs_b = jnp.where(chain < bsz, 0.0, lhs).astype(BF16)
    ys = []
    for g in range(S5_SUPER):
        cs = slice(g * cw, (g + 1) * cw)
        re = slice(g * sw, (g + 1) * sw)
        im = slice(S5_NS + g * sw, S5_NS + (g + 1) * sw)
        bu = _dot(jnp.concatenate([lhs_f[:, cs], lhs_b[:, cs]], axis=1), wb_ref[g])
        x_scr[:, re] = bu[:, :sw]
        x_scr[:, im] = bu[:, sw:]
        ar, ai = a_ref[:, re], a_ref[:, im]

        def step(j, carry):
            sr, si = carry
            r0 = pl.multiple_of(j * nchain, nchain)
            nr = ar * sr - ai * si + x_scr[pl.ds(r0, nchain), re]
            ni = ar * si + ai * sr + x_scr[pl.ds(r0, nchain), im]
            x_scr[pl.ds(r0, nchain), re] = nr
            x_scr[pl.ds(r0, nchain), im] = ni
            return nr, ni

        sr, si = lax.fori_loop(0, T, step, (st_scr[:, re], st_scr[:, im]))
        st_scr[:, re] = sr
        st_scr[:, im] = si
        wc = wc_ref[g]
        ys.append(_dot(x_scr[:, re].astype(BF16), wc[:sw]) + _dot(x_scr[:, im].astype(BF16), wc[sw:]))
    y = _dot_sel(pmt_ref[...], jnp.concatenate(ys, axis=1))
    for c in range(nchain):
        y_ref[c // bsz, c % bsz] = y[c * T:(c + 1) * T]


def _s5_scan(p, pm, pmt, wb, atab, wc, bsz, seq, ctx_len):
    T = S5_T
    nchain = 2 * bsz
    ncc, nlc = ctx_len // T, seq // T
    nsteps = ncc + nlc
    rows = nchain * T
    const = lambda shape: pl.BlockSpec(shape, lambda i: (0,) * len(shape))
    u_specs = []
    for c in range(nchain):
        rev = c >= bsz
        u_specs.append(pl.BlockSpec(
            (T, W), functools.partial(lambda i, b, rev: (_chunk_index(b, i, rev, ncc, nlc, bsz), P_S5 // W),
                                      b=c % bsz, rev=rev)))
    return pl.pallas_call(
        functools.partial(_s5_body, nchain=nchain, bsz=bsz),
        name="s5_scan",
        grid=(nsteps,),
        in_specs=u_specs + [const((rows, rows)), const((rows, rows)), const((S5_SUPER, 2 * W // S5_SUPER, 2 * S5_NS // S5_SUPER)),
                            const((nchain, 2 * S5_NS)), const((S5_SUPER, 2 * S5_NS // S5_SUPER, W // S5_SUPER))],
        out_specs=pl.BlockSpec((2, bsz, None, T, W), lambda i: (0, 0, i, 0, 0)),
        out_shape=jax.ShapeDtypeStruct((2, bsz, nsteps, T, W), F32),
        scratch_shapes=[pltpu.VMEM((rows, 2 * S5_NS), F32), pltpu.VMEM((nchain, 2 * S5_NS), F32)],
        compiler_params=_cparams("arbitrary"),
    )(*([p] * nchain), pm, pmt, wb, atab, wc)


def _s5_fin_body(yf_ref, yb_ref, u_ref, d_ref, w_ref, b_ref, o_ref, *, nchunk):
    yf = yf_ref[...].reshape(nchunk * S5_T, W)
    yb = jnp.concatenate([yb_ref[nchunk - 1 - q] for q in range(nchunk)], axis=0)
    y = yf + yb + d_ref[...] * u_ref[...]
    g = _gelu_tanh(y)
    o_ref[...] = g * _sigmoid(_dot(g.astype(BF16), w_ref[...].astype(BF16)) + b_ref[...])


def _s5_finish(ys, p, d_skip, glu_w, glu_b, rows, bsz, seq, ctx_len):
    tile = ctx_len
    g = tile // S5_T
    nlt = seq // tile
    n_lat = bsz * nlt
    samp = lambda i: jnp.where(i < n_lat, i // nlt, i - n_lat)
    fblk = lambda i: jnp.where(i < n_lat, 1 + i % nlt, 0)
    bblk = lambda i: jnp.where(i < n_lat, nlt - i % nlt, 0)
    vec = pl.BlockSpec((1, W), lambda i: (0, 0))
    return pl.pallas_call(
        functools.partial(_s5_fin_body, nchunk=g),
        name="s5_finish",
        grid=(rows // tile,),
        in_specs=[pl.BlockSpec((None, None, g, S5_T, W), lambda i: (0, samp(i), fblk(i), 0, 0)),
                  pl.BlockSpec((None, None, g, S5_T, W), lambda i: (1, samp(i), bblk(i), 0, 0)),
                  pl.BlockSpec((tile, W), lambda i: (i, P_S5 // W)), vec,
                  pl.BlockSpec((W, W), lambda i: (0, 0)), vec],
        out_specs=pl.BlockSpec((tile, W), lambda i: (i, 0)),
        out_shape=jax.ShapeDtypeStruct((rows, W), F32),
        compiler_params=_cparams("parallel"),
    )(ys, ys, p, d_skip.reshape(1, W), glu_w, glu_b.reshape(1, W))


def _block_diag(m):
    g, a, b = m.shape
    eye = jnp.eye(g, dtype=m.dtype)
    return (eye[:, None, :, None] * m[:, :, None, :]).reshape(g * a, g * b)


def _s5_branch(p, rows, bsz, seq, ctx_len, a_re, a_im, log_dt, b_re, b_im, c_re, c_im, d_skip, glu_w, glu_b):
    disc = [_s5_discretise(a_re[d], a_im[d], log_dt[d], b_re, b_im) for d in range(2)]
    gs = S5_GROUPS // S5_SUPER
    def bmat(v):
        v = v.reshape(S5_SUPER, gs, S5_STATE, S5_GROUP).transpose(0, 1, 3, 2)
        return jnp.stack([_block_diag(v[s]) for s in range(S5_SUPER)])
    wb = jnp.concatenate([jnp.concatenate([bmat(disc[d][2]), bmat(disc[d][3])], axis=2) for d in range(2)],
                         axis=1).astype(BF16)
    arow = lambda d: jnp.concatenate([disc[d][0].reshape(1, S5_NS), disc[d][1].reshape(1, S5_NS)], axis=1)
    atab = jnp.concatenate([jnp.broadcast_to(arow(0), (bsz, 2 * S5_NS)),
                            jnp.broadcast_to(arow(1), (bsz, 2 * S5_NS))], axis=0)
    def cmat(v):
        v = v.reshape(S5_SUPER, gs, S5_GROUP, S5_STATE).transpose(0, 1, 3, 2)
        return jnp.stack([_block_diag(v[s]) for s in range(S5_SUPER)])
    wc = jnp.concatenate([cmat(c_re), -cmat(c_im)], axis=1).astype(BF16)
    T, nchain = S5_T, 2 * bsz
    r = jnp.arange(nchain * T, dtype=I32)
    j, c = r // nchain, r % nchain
    src = c * T + jnp.where(c < bsz, j, T - 1 - j)
    pm = (src[:, None] == r[None, :])
    ys = _s5_scan(p, pm.astype(BF16), pm.T.astype(BF16), wb, atab, wc, bsz, seq, ctx_len)
    return _s5_finish(ys, p, d_skip, glu_w, glu_b, rows, bsz, seq, ctx_len)


def _merge_body(y0l, y0c, y1, y2, y3, g0, g1, g2, g3, w0, w1, w2, w3, o_ref, *, n_lat_tiles):
    hy = jnp.where(pl.program_id(0) < n_lat_tiles, y0l[...], y0c[...])
    acc = _sigmoid(g0[...]) * _dot(hy.astype(BF16), w0[...])
    for y, g, w in ((y1, g1, w1), (y2, g2, w2), (y3, g3, w3)):
        acc = acc + _sigmoid(g[...]) * _dot(y[...].astype(BF16), w[...])
    o_ref[...] = acc.astype(BF16)


def _merge(hy_lat, hy_ctx, ys, p, wb, rows):
    nl = hy_lat.shape[0]
    tm, tn = _tile(math.gcd(nl, hy_ctx.shape[0]), 512), 512
    n_lat = nl // tm
    ysp = pl.BlockSpec((tm, W), lambda i, j: (i, 0))
    gsp = lambda k: pl.BlockSpec((tm, tn), lambda i, j: (i, (P_GATE + k * D_MODEL) // tn + j))
    wsp = lambda k: pl.BlockSpec((None, W, tn), lambda i, j: (k, 0, j))
    return pl.pallas_call(
        functools.partial(_merge_body, n_lat_tiles=n_lat),
        name="merge",
        grid=(rows // tm, D_MODEL // tn),
        in_specs=[pl.BlockSpec((tm, W), lambda i, j: (jnp.minimum(i, n_lat - 1), 0)),
                  pl.BlockSpec((tm, W), lambda i, j: (jnp.maximum(i - n_lat, 0), 0))]
                 + [ysp] * 3 + [gsp(k) for k in range(4)] + [wsp(k) for k in range(4)],
        out_specs=pl.BlockSpec((tm, tn), lambda i, j: (i, j)),
        out_shape=jax.ShapeDtypeStruct((rows, D_MODEL), BF16),
        compiler_params=_cparams("parallel", "parallel"),
    )(hy_lat, hy_ctx, *ys, p, p, p, p, wb, wb, wb, wb)


def _outproj_body(a_ref, w_ref, xl_ref, xc_ref, mod_ref, o_ref, *, n_lat_tiles):
    y = mod_ref[2:3, :] * _dot(a_ref[...], w_ref[...])
    i = pl.program_id(0)

    @pl.when(i < n_lat_tiles)
    def _():
        o_ref[...] = xl_ref[...] + y

    @pl.when(i >= n_lat_tiles)
    def _():
        o_ref[...] = xc_ref[...] + y


def _outproj(acc, w, xl, xc, mods, seq, bsz):
    rows = acc.shape[0]
    nl, nc = xl.shape[0], xc.shape[0]
    tm = _tile(math.gcd(nl, nc), 512)
    tn = D_MODEL
    n_lat = nl // tm
    return pl.pallas_call(
        functools.partial(_outproj_body, n_lat_tiles=n_lat),
        name="outproj",
        grid=(rows // tm, D_MODEL // tn),
        in_specs=[
            pl.BlockSpec((tm, D_MODEL), lambda i, j: (i, 0)),
            pl.BlockSpec((D_MODEL, tn), lambda i, j: (0, j)),
            pl.BlockSpec((tm, tn), lambda i, j: (jnp.minimum(i, n_lat - 1), j)),
            pl.BlockSpec((tm, tn), lambda i, j: (jnp.maximum(i - n_lat, 0), j)),
            pl.BlockSpec((None, 6, tn), lambda i, j: (_mod_index(i, tm, n_lat, seq, bsz), 0, j)),
        ],
        out_specs=pl.BlockSpec((tm, tn), lambda i, j: (i, j)),
        out_shape=jax.ShapeDtypeStruct((rows, D_MODEL), F32),
        compiler_params=_cparams("parallel", "parallel"),
    )(acc, w, xl, xc, mods)


def _router_body(x_ref, g_ref, mod_ref, wr_ref, h_ref, p_ref):
    h = _norm_mod(x_ref[...], g_ref[...], mod_ref[3:4, :], mod_ref[4:5, :])
    h_ref[...] = h.astype(BF16)
    logits = _dot3(h, wr_ref[...])
    lane = lax.broadcasted_iota(I32, logits.shape, 1)
    logits = jnp.where(lane < N_EXPERTS, logits, -jnp.inf)
    e = jnp.exp(logits - jnp.max(logits, axis=-1, keepdims=True))
    p_ref[...] = e / jnp.sum(e, axis=-1, keepdims=True)


def _router(xm, g, mods, w_router, n_lat_rows, seq, bsz):
    rows = xm.shape[0]
    tm = _tile(math.gcd(n_lat_rows, seq), 512)
    n_lat = n_lat_rows // tm
    wr = jnp.pad(w_router, ((0, 0), (0, LANES - N_EXPERTS)))
    return pl.pallas_call(
        _router_body,
        name="moe_router",
        grid=(rows // tm,),
        in_specs=[
            pl.BlockSpec((tm, D_MODEL), lambda i: (i, 0)),
            pl.BlockSpec((1, D_MODEL), lambda i: (0, 0)),
            pl.BlockSpec((None, 6, D_MODEL), lambda i: (_mod_index(i, tm, n_lat, seq, bsz), 0, 0)),
            pl.BlockSpec((D_MODEL, LANES), lambda i: (0, 0)),
        ],
        out_specs=[pl.BlockSpec((tm, D_MODEL), lambda i: (i, 0)), pl.BlockSpec((tm, LANES), lambda i: (i, 0))],
        out_shape=[jax.ShapeDtypeStruct((rows, D_MODEL), BF16), jax.ShapeDtypeStruct((rows, LANES), F32)],
        compiler_params=_cparams("parallel"),
    )(xm, g, mods, wr)


def _route_geometry(n):
    cap = EC_CAPACITY * n // N_EXPERTS
    blk = min(n, ROUTE_BLOCK)
    nblk = n // blk
    nb1 = -(-(nblk + 1) // 8) * 8
    st = min(cap, LANES)
    return cap, blk, nblk, nb1, st


def _topk_body(p_ref, o_ref, cnt_ref, *, n, cap, blk):
    bits = pltpu.bitcast(p_ref[...], I32)
    capf = float(cap)

    def count(mask):
        return jnp.sum(jnp.where(mask, 1.0, 0.0), axis=0, keepdims=True)

    def vstep(i, thr):
        cand = thr | lax.shift_left(jnp.int32(1), 30 - i)
        return jnp.where(count(bits >= cand) >= capf, cand, thr)

    thr = lax.fori_loop(0, 31, vstep, jnp.zeros((1, LANES), I32))
    gt = bits > thr
    eq = bits == thr
    need = capf - count(gt)
    t = lax.broadcasted_iota(I32, bits.shape, 0)
    nbits = max(1, (n - 1).bit_length())

    def istep(i, j):
        cand = j + lax.shift_left(jnp.int32(1), nbits - 1 - i)
        return jnp.where(count(eq & (t < cand)) < need, cand, j)

    jmax = lax.fori_loop(0, nbits, istep, jnp.zeros((1, LANES), I32))
    sel = gt | (eq & (t <= jmax))
    self32 = jnp.where(sel, 1.0, 0.0)
    r = lax.broadcasted_iota(I32, (blk, blk), 0)
    c = lax.broadcasted_iota(I32, (blk, blk), 1)
    lower = jnp.where(c < r, 1.0, 0.0).astype(BF16)
    carry = jnp.zeros((1, LANES), F32)
    cnt_ref[...] = jnp.zeros_like(cnt_ref)
    for i in range(n // blk):
        sb = self32[i * blk:(i + 1) * blk]
        rank = _dot(lower, sb.astype(BF16)) + carry
        o_ref[i * blk:(i + 1) * blk, :] = jnp.where(sb > 0.0, rank, -1.0).astype(I32)
        carry = carry + jnp.sum(sb, axis=0, keepdims=True)
        cnt_ref[i + 1:i + 2, :] = carry.astype(I32)


def _topk(probs, row0, n, nsets, name):
    cap, blk, nblk, nb1, _ = _route_geometry(n)
    return pl.pallas_call(
        functools.partial(_topk_body, n=n, cap=cap, blk=blk),
        name=name,
        grid=(nsets,),
        in_specs=[pl.BlockSpec((n, LANES), lambda s: (row0 // n + s, 0))],
        out_specs=[pl.BlockSpec((n, LANES), lambda s: (s, 0)), pl.BlockSpec((None, nb1, LANES), lambda s: (s, 0, 0))],
        out_shape=[jax.ShapeDtypeStruct((nsets * n, LANES), I32), jax.ShapeDtypeStruct((nsets, nb1, LANES), I32)],
        compiler_params=_cparams("parallel"),
    )(probs)


def _gather_body(cnt_ref, slot_ref, p_ref, h_ref, xs_ref, gate_ref, acc, gacc, *, cap, blk, nblk, nb1, st):
    s = pl.program_id(0)
    first_half = pl.program_id(1) == 0
    e = pl.program_id(2)
    bounds = [cnt_ref[(s * nb1 + j) * N_EXPERTS + e] for j in range(nblk + 1)]
    lane = lax.broadcasted_iota(I32, (st, LANES), 1)
    for q in range(cap // st):
        lo, hi = q * st, (q + 1) * st
        b_lo = sum((bounds[j + 1] <= lo).astype(I32) for j in range(nblk))
        b_hi = sum((bounds[j] < hi).astype(I32) for j in range(nblk))
        rr = lax.broadcasted_iota(I32, (st, blk), 0) + lo
        acc[...] = jnp.zeros_like(acc)
        gacc[...] = jnp.zeros_like(gacc)

        def body(j, _):
            r0 = pl.multiple_of(j * blk, blk)
            onehot = jnp.where(slot_ref[e, pl.ds(j, 1), :] == rr, 1.0, 0.0).astype(BF16)
            acc[...] += _dot(onehot, h_ref[pl.ds(r0, blk), :])

            @pl.when(first_half)
            def _():
                gacc[...] += _dot_sel(onehot, p_ref[pl.ds(r0, blk), :])

            return 0

        lax.fori_loop(b_lo, b_hi, body, 0)
        xs_ref[lo:hi, :] = acc[...].astype(BF16)

        @pl.when(first_half)
        def _():
            gate_ref[lo:hi, :] = jnp.sum(jnp.where(lane == e, gacc[...], 0.0), axis=1, keepdims=True)


def _gather(slot_col, counts, probs, h2, row0, n, nsets, name):
    cap, blk, nblk, nb1, st = _route_geometry(n)
    dh = D_MODEL // 2
    slot_row = slot_col.reshape(nsets, nblk, blk, LANES)[..., :N_EXPERTS].transpose(0, 3, 1, 2)
    cnt = counts[:, :, :N_EXPERTS].reshape(-1)
    grid_spec = pltpu.PrefetchScalarGridSpec(
        num_scalar_prefetch=1,
        grid=(nsets, 2, N_EXPERTS),
        in_specs=[
            pl.BlockSpec((None, N_EXPERTS, nblk, blk), lambda s, k, e, c: (s, 0, 0, 0)),
            pl.BlockSpec((n, LANES), lambda s, k, e, c: (row0 // n + s, 0)),
            pl.BlockSpec((n, dh), lambda s, k, e, c: (row0 // n + s, k)),
        ],
        out_specs=[pl.BlockSpec((None, cap, dh), lambda s, k, e, c: (e, s, k)),
                   pl.BlockSpec((None, cap, 1), lambda s, k, e, c: (jnp.where(k == 0, e, N_EXPERTS - 1), s, 0))],
        scratch_shapes=[pltpu.VMEM((st, dh), F32), pltpu.VMEM((st, LANES), F32)],
    )
    return pl.pallas_call(
        functools.partial(_gather_body, cap=cap, blk=blk, nblk=nblk, nb1=nb1, st=st),
        name=name,
        grid_spec=grid_spec,
        out_shape=[jax.ShapeDtypeStruct((N_EXPERTS, nsets * cap, D_MODEL), BF16),
                   jax.ShapeDtypeStruct((N_EXPERTS, nsets * cap, 1), F32)],
        compiler_params=_cparams("parallel", "arbitrary", "arbitrary"),
    )(cnt, slot_row, probs, h2)


def _ffn_body(*refs, with_ctx, ml):
    if with_ctx:
        xl_ref, gl_ref, xc_ref, gc_ref, wg_ref, wu_ref, wd_ref, yl_ref, yc_ref, acc = refs
        xs = jnp.concatenate([xl_ref[...], xc_ref[...]], axis=0)
    else:
        xl_ref, gl_ref, wg_ref, wu_ref, wd_ref, yl_ref, acc = refs
        xs = xl_ref[...]
    f = pl.program_id(2)

    @pl.when(f == 0)
    def _():
        acc[...] = jnp.zeros_like(acc)

    hid = _silu(_dot(xs, wg_ref[...].astype(BF16))) * _dot(xs, wu_ref[...].astype(BF16))
    acc[...] += _dot(hid.astype(BF16), wd_ref[...].astype(BF16))

    @pl.when(f == pl.num_programs(2) - 1)
    def _():
        yl_ref[...] = (acc[0:ml, :] * gl_ref[...]).astype(BF16)
        if with_ctx:
            yc_ref[...] = (acc[ml:, :] * gc_ref[...]).astype(BF16)


def _ffn(xs_l, g_l, xs_c, g_c, w_gate, w_up, w_down, layer):
    with_ctx = xs_c is not None
    nsplit = 2
    ml = xs_l.shape[1] // nsplit
    mc = xs_c.shape[1] // nsplit if with_ctx else 0
    fc = 256
    row = lambda m, last: pl.BlockSpec((None, m, last), lambda e, s, f: (e, s, 0))
    in_specs = [row(ml, D_MODEL), row(ml, 1)]
    args = [xs_l, g_l]
    out_specs = [row(ml, D_MODEL)]
    out_shape = [jax.ShapeDtypeStruct(xs_l.shape, BF16)]
    if with_ctx:
        in_specs += [row(mc, D_MODEL), row(mc, 1)]
        args += [xs_c, g_c]
        out_specs.append(row(mc, D_MODEL))
        out_shape.append(jax.ShapeDtypeStruct(xs_c.shape, BF16))
    in_specs += [pl.BlockSpec((None, None, D_MODEL, fc), lambda e, s, f: (layer, e, 0, f)),
                 pl.BlockSpec((None, None, D_MODEL, fc), lambda e, s, f: (layer, e, 0, f)),
                 pl.BlockSpec((None, None, fc, D_MODEL), lambda e, s, f: (layer, e, f, 0))]
    args += [w_gate, w_up, w_down]
    out = pl.pallas_call(
        functools.partial(_ffn_body, with_ctx=with_ctx, ml=ml),
        name="moe_ffn",
        grid=(N_EXPERTS, nsplit, D_EXPERT // fc),
        in_specs=in_specs,
        out_specs=out_specs,
        out_shape=out_shape,
        scratch_shapes=[pltpu.VMEM((ml + mc, D_MODEL), F32)],
        compiler_params=_cparams("parallel", "parallel", "arbitrary"),
    )(*args)
    return (out[0], out[1]) if with_ctx else (out[0], None)


def _scatter_body(cnt_ref, slot_ref, y_ref, x_ref, mod_ref, o_ref, acc, *, st, nb1, bpt):
    s, t, e = pl.program_id(0), pl.program_id(1), pl.program_id(2)

    @pl.when(e == 0)
    def _():
        acc[...] = jnp.zeros_like(acc)

    lo = cnt_ref[(s * nb1 + t * bpt) * N_EXPERTS + e]
    hi = cnt_ref[(s * nb1 + (t + 1) * bpt) * N_EXPERTS + e]
    shift = st.bit_length() - 1
    k_lo = lax.shift_right_logical(lo, shift)
    k_hi = lax.shift_right_logical(hi + (st - 1), shift)
    slot = slot_ref[...]
    lane = lax.broadcasted_iota(I32, slot.shape, 1)
    col = jnp.sum(jnp.where(lane == e, slot, 0).astype(F32), axis=1, keepdims=True)
    r = lax.broadcasted_iota(I32, (slot.shape[0], st), 1).astype(F32)

    def body(kb, _):
        k0 = pl.multiple_of(kb * st, st)
        onehot = jnp.where(col == r + k0.astype(F32), 1.0, 0.0).astype(BF16)
        acc[...] += _dot(onehot, y_ref[pl.ds(k0, st), :])
        return 0

    lax.fori_loop(k_lo, k_hi, body, 0)

    @pl.when(e == pl.num_programs(2) - 1)
    def _():
        o_ref[...] = x_ref[...] + mod_ref[5:6, :] * acc[...]


def _scatter(slot_col, counts, y, xm, mods, row0, n, nsets, mod_of_set, name):
    cap, blk, nblk, nb1, st = _route_geometry(n)
    tt = _tile(n, 1024)
    nt = n // tt
    cnt = counts[:, :, :N_EXPERTS].reshape(-1)
    grid_spec = pltpu.PrefetchScalarGridSpec(
        num_scalar_prefetch=1,
        grid=(nsets, nt, N_EXPERTS),
        in_specs=[
            pl.BlockSpec((tt, LANES), lambda s, t, e, c: (s * nt + t, 0)),
            pl.BlockSpec((None, cap, D_MODEL), lambda s, t, e, c: (e, s, 0)),
            pl.BlockSpec((tt, D_MODEL), lambda s, t, e, c: (row0 // tt + s * nt + t, 0)),
            pl.BlockSpec((None, 6, D_MODEL), lambda s, t, e, c: (mod_of_set(s), 0, 0)),
        ],
        out_specs=pl.BlockSpec((tt, D_MODEL), lambda s, t, e, c: (s * nt + t, 0)),
        scratch_shapes=[pltpu.VMEM((tt, D_MODEL), F32)],
    )
    return pl.pallas_call(
        functools.partial(_scatter_body, st=st, nb1=nb1, bpt=tt // blk),
        name=name,
        grid_spec=grid_spec,
        out_shape=jax.ShapeDtypeStruct((nsets * n, D_MODEL), F32),
        compiler_params=_cparams("parallel", "parallel", "arbitrary"),
    )(cnt, slot_col, y, xm, mods)


def _moe(xm, g, mods, w_router, w_gate, w_up, w_down, layer, n_lat_rows, bsz, seq, ctx_len, with_ctx):
    h2, probs = _router(xm, g, mods, w_router, n_lat_rows, seq, bsz)

    def route(row0, n, tag):
        slot_col, counts = _topk(probs, row0, n, bsz, "moe_topk" + tag)
        xs, gate = _gather(slot_col, counts, probs, h2, row0, n, bsz, "moe_gather" + tag)
        return slot_col, counts, xs, gate

    slot_l, cnt_l, xs_l, gate_l = route(0, seq, "_lat")
    if with_ctx:
        slot_c, cnt_c, xs_c, gate_c = route(n_lat_rows, ctx_len, "_ctx")
    else:
        xs_c = gate_c = None
    y_l, y_c = _ffn(xs_l, gate_l, xs_c, gate_c, w_gate, w_up, w_down, layer)
    xl = _scatter(slot_l, cnt_l, y_l, xm, mods, 0, seq, bsz, lambda s: s, "moe_scatter_lat")
    xc = None
    if with_ctx:
        xc = _scatter(slot_c, cnt_c, y_c, xm, mods, n_lat_rows, ctx_len, bsz, lambda s: bsz, "moe_scatter_ctx")
    return xl, xc


def _final_norm_body(x_ref, g_ref, o_ref):
    x = x_ref[...]
    o_ref[...] = x * lax.rsqrt(jnp.mean(x * x, axis=-1, keepdims=True) + EPS) * g_ref[...]


def _final_norm(x, g):
    rows, d = x.shape
    tm = _tile(rows, 512)
    return pl.pallas_call(
        _final_norm_body,
        name="final_norm",
        grid=(rows // tm,),
        in_specs=[pl.BlockSpec((tm, d), lambda i: (i, 0)), pl.BlockSpec((1, d), lambda i: (0, 0))],
        out_specs=pl.BlockSpec((tm, d), lambda i: (i, 0)),
        out_shape=jax.ShapeDtypeStruct((rows, d), F32),
        compiler_params=_cparams("parallel"),
    )(x, g)


def _reorder_w_in(w):
    pad = jnp.zeros((w.shape[0], P_WIDTH - P_SMALL - N_SMALL), w.dtype)
    return jnp.concatenate([w[:, 0:3584], w[:, 3600:5648], w[:, 5664:13856], w[:, 3584:3600], w[:, 5648:5664], pad],
                           axis=1).astype(BF16)


def _mixer(xl, xc, mods, lw, tabs_lat, tabs_ctx, bsz, seq, ctx_len, with_ctx_out):
    n_lat_rows = bsz * seq
    rows_all = n_lat_rows + bsz * ctx_len
    rows = rows_all if with_ctx_out else n_lat_rows
    h = _normmod(xl, xc, lw["norm_mix"], mods, seq, bsz)
    p = _matmul(h, lw["w_in"], 1024, 1024, "inproj")
    small_t = _small_t(p)
    ones = lambda n: jnp.ones((1, n), F32)
    u_hy = _short_conv(p, P_HY, 3 * W, lw["hy_conv_w"], lw["hy_conv_b"].reshape(1, -1), ones(3 * W), False,
                       n_lat_rows, ctx_len, "conv_hyena")
    hy_args = (lw["hy_w1"], lw["hy_b1"], lw["hy_fr1"], lw["hy_w2"], lw["hy_b2"], lw["hy_fr2"], lw["hy_w3"],
               lw["hy_decay"])

    def hy_filter(n, tabs, tag):
        gp, gm = _hy_filters(n, *hy_args, "hy_filter" + tag)
        return (_matmul(tabs[0], gp, 512, 512, "hy_spec_c" + tag), _matmul(tabs[1], gm, 512, 512, "hy_spec_s" + tag),
                lw["hy_bias"])

    hy_lat = _hyena(u_hy, 0, seq, bsz, tabs_lat, hy_filter(seq, tabs_lat, "_lat"), "_lat")
    hy_ctx = hy_lat
    if with_ctx_out:
        hy_ctx = _hyena(u_hy, n_lat_rows, ctx_len, bsz, tabs_ctx, hy_filter(ctx_len, tabs_ctx, "_ctx"), "_ctx")
    post = jnp.concatenate([jnp.ones((1, W), F32), jnp.full((1, W), ML_HD ** -0.5, F32)], axis=1)
    qk = _short_conv(p, P_MLQK, 2 * W, lw["ml_conv_w"], lw["ml_conv_b"].reshape(1, -1), post, True, n_lat_rows,
                     ctx_len, "conv_mlstm")
    hf, hb = _mlstm_scan(qk, p, small_t, lw["ml_gate_b"].reshape(16), bsz, seq, ctx_len)
    y_ml = _mlstm_finish(hf, hb, p, lw["ml_norm"].reshape(1, W), rows)
    y_s5 = _s5_branch(p, rows, bsz, seq, ctx_len, lw["s5_a_re"], lw["s5_a_im"], lw["s5_log_dt"], lw["s5_b_re"],
                      lw["s5_b_im"], lw["s5_c_re"], lw["s5_c_im"], lw["s5_d"], lw["s5_glu_w"], lw["s5_glu_b"])
    xbc = _short_conv(p, P_XBC, 2 * W, lw["ssd_conv_w"], lw["ssd_conv_b"].reshape(1, -1), ones(2 * W), True,
                      n_lat_rows, ctx_len, "conv_ssd")
    sf, sb = _ssd_scan(xbc, p, small_t, lw["ssd_dt_bias"].reshape(16), lw["ssd_a_log"].reshape(16), bsz, seq, ctx_len)
    y_ssd = _ssd_finish(sf, sb, xbc, p, lw["ssd_d"], lw["ssd_norm"].reshape(1, W), rows)
    acc = _merge(hy_lat, hy_ctx, (y_ml, y_s5, y_ssd), p, lw["w_branch"], rows)
    return _outproj(acc, lw["w_out"], xl, xc, mods, seq, bsz)


_PER_LAYER = ("ada_w", "ada_b", "norm_mix", "norm_ffn", "w_in", "hy_conv_w", "hy_conv_b", "hy_w1", "hy_b1", "hy_fr1",
              "hy_w2", "hy_b2", "hy_fr2", "hy_w3", "hy_decay", "hy_bias", "ml_conv_w", "ml_conv_b", "ml_gate_b",
              "ml_norm", "s5_a_re", "s5_a_im", "s5_log_dt", "s5_b_re", "s5_b_im", "s5_c_re", "s5_c_im", "s5_d",
              "s5_glu_w", "s5_glu_b", "ssd_conv_w", "ssd_conv_b", "ssd_dt_bias", "ssd_a_log", "ssd_d", "ssd_norm",
              "w_branch", "w_out", "w_router", "w_gate", "w_up", "w_down")


def kernel(x, c, ctx, c_ctx, ada_w, ada_b, norm_mix, norm_ffn, w_in, hy_conv_w, hy_conv_b, hy_w1, hy_b1, hy_fr1, hy_w2, hy_b2, hy_fr2, hy_w3, hy_decay, hy_bias, ml_conv_w, ml_conv_b, ml_gate_b, ml_norm, s5_a_re, s5_a_im, s5_log_dt, s5_b_re, s5_b_im, s5_c_re, s5_c_im, s5_d, s5_glu_w, s5_glu_b, ssd_conv_w, ssd_conv_b, ssd_dt_bias, ssd_a_log, ssd_d, ssd_norm, w_branch, w_out, w_router, w_gate, w_up, w_down, final_norm):
    stacked = dict(zip(_PER_LAYER, (ada_w, ada_b, norm_mix, norm_ffn, w_in, hy_conv_w, hy_conv_b, hy_w1, hy_b1, hy_fr1,
                                    hy_w2, hy_b2, hy_fr2, hy_w3, hy_decay, hy_bias, ml_conv_w, ml_conv_b, ml_gate_b,
                                    ml_norm, s5_a_re, s5_a_im, s5_log_dt, s5_b_re, s5_b_im, s5_c_re, s5_c_im, s5_d,
                                    s5_glu_w, s5_glu_b, ssd_conv_w, ssd_conv_b, ssd_dt_bias, ssd_a_log, ssd_d, ssd_norm,
                                    w_branch, w_out, w_router, w_gate, w_up, w_down)))
    bsz, seq, d = x.shape
    ctx_len = ctx.shape[1]
    depth = ada_w.shape[0]
    assert d == D_MODEL and 2 * bsz <= 8 and seq % ctx_len == 0 and ctx_len % SCAN_BLOCK == 0
    xl = x.reshape(bsz * seq, d)
    xc = ctx.reshape(bsz * ctx_len, d)
    cc = jnp.zeros((8, d), F32).at[:bsz].set(c).at[bsz].set(c_ctx)
    tabs_lat = _dft_tables(seq)
    tabs_ctx = _dft_tables(ctx_len)
    for i in range(depth):
        last = i == depth - 1
        lw = {k: v[i] for k, v in stacked.items() if k not in ("ada_w", "w_gate", "w_up", "w_down")}
        lw["w_in"] = _reorder_w_in(lw["w_in"])
        lw["norm_mix"] = lw["norm_mix"].reshape(1, d)
        lw["norm_ffn"] = lw["norm_ffn"].reshape(1, d)
        lw["w_branch"] = lw["w_branch"].astype(BF16)
        lw["w_out"] = lw["w_out"].astype(BF16)
        mods = _ada(cc, ada_w, lw["ada_b"].reshape(1, -1), i).reshape(8, 6, d)
        xm = _mixer(xl, xc, mods, lw, tabs_lat, tabs_ctx, bsz, seq, ctx_len, not last)
        xl, xc_new = _moe(xm, lw["norm_ffn"], mods, lw["w_router"], w_gate, w_up, w_down, i,
                          bsz * seq, bsz, seq, ctx_len, not last)
        if not last:
            xc = xc_new
    return _final_norm(xl, final_norm.reshape(1, d)).reshape(bsz, seq, d)
```

```python
import functools
import math

import jax
import jax.numpy as jnp
from jax import lax
from jax.experimental import pallas as pl
from jax.experimental.pallas import tpu as pltpu

F32 = jnp.float32
BF16 = jnp.bfloat16
I32 = jnp.int32

D_MODEL = 2048
W = 512
GRID_W = 64
EPS = 1e-6
HY_EMB = 33
HY_BANDS = 16
HY_FFN = 64
ML_HEADS = 4
ML_HD = 128
ML_CHUNK = 64
S5_GROUP = 16
S5_GROUPS = 32
S5_STATE = 64
S5_NS = S5_GROUPS * S5_STATE
S5_SUPER = 4
S5_T = 64
SSD_HD = 64
SSD_HEADS = 8
SSD_GROUPS = 2
SSD_STATE = 128
SSD_CHUNK = 128
N_EXPERTS = 16
EC_CAPACITY = 2
D_EXPERT = 1536
SCAN_BLOCK = 128
ROUTE_BLOCK = 256

P_HY = 0
P_MLQK = 1536
P_MLV = 2560
P_MLO = 3072
P_S5 = 3584
P_SSDZ = 4096
P_XBC = 4608
P_GATE = 5632
P_SMALL = 13824
P_WIDTH = 14336
N_SMALL = 32

LANES = 128
VMEM_LIMIT_BYTES = 56 * 1024 * 1024


def _cparams(*sem):
    return pltpu.CompilerParams(dimension_semantics=sem, vmem_limit_bytes=VMEM_LIMIT_BYTES)


def _dot(a, b):
    return jnp.dot(a, b, preferred_element_type=F32)


def _dot_nt(a, b):
    return lax.dot_general(a, b, (((1,), (1,)), ((), ())), preferred_element_type=F32)


def _split3(x):
    hi = x.astype(BF16)
    r = x - hi.astype(F32)
    mid = r.astype(BF16)
    lo = (r - mid.astype(F32)).astype(BF16)
    return hi, mid, lo


def _dot_sel(m01, x):
    hi, mid, lo = _split3(x)
    return _dot(m01, hi) + _dot(m01, mid) + _dot(m01, lo)


def _sel_dot(x, m01):
    hi, mid, lo = _split3(x)
    return _dot(hi, m01) + _dot(mid, m01) + _dot(lo, m01)


def _dot3(a, b):
    ah = a.astype(BF16)
    al = (a - ah.astype(F32)).astype(BF16)
    bh = b.astype(BF16)
    bl = (b - bh.astype(F32)).astype(BF16)
    return _dot(ah, bh) + _dot(al, bh) + _dot(ah, bl)


def _sigmoid(x):
    return 1.0 / (1.0 + jnp.exp(-x))


def _silu(x):
    return x * _sigmoid(x)


def _softplus(x):
    return jnp.maximum(x, 0.0) + jnp.log(1.0 + jnp.exp(-jnp.abs(x)))


def _log_sigmoid(x):
    return jnp.minimum(x, 0.0) - jnp.log(1.0 + jnp.exp(-jnp.abs(x)))


def _gelu_tanh(x):
    return 0.5 * x * (1.0 + jnp.tanh(math.sqrt(2.0 / math.pi) * (x + 0.044715 * (x * x * x))))


def _tile(n, pref):
    t = min(n, pref)
    while n % t:
        t //= 2
    return t


def _mod_index(i, tm, n_lat_tiles, seq, bsz):
    return jnp.where(i < n_lat_tiles, (i * tm) // seq, bsz)


def _mm_body(a_ref, b_ref, o_ref):
    o_ref[...] = _dot(a_ref[...].astype(BF16), b_ref[...].astype(BF16)).astype(o_ref.dtype)


def _matmul(a, b, tm, tn, name, out_dtype=F32):
    m, k = a.shape
    n = b.shape[1]
    tm, tn = _tile(m, tm), _tile(n, tn)
    return pl.pallas_call(
        _mm_body,
        name=name,
        grid=(m // tm, n // tn),
        in_specs=[pl.BlockSpec((tm, k), lambda i, j: (i, 0)), pl.BlockSpec((k, tn), lambda i, j: (0, j))],
        out_specs=pl.BlockSpec((tm, tn), lambda i, j: (i, j)),
        out_shape=jax.ShapeDtypeStruct((m, n), out_dtype),
        compiler_params=_cparams("parallel", "parallel"),
    )(a, b)


def _ada_body(c_ref, w_ref, b_ref, o_ref):
    c = c_ref[...]
    o_ref[...] = _dot3(_silu(c), w_ref[...]) + b_ref[...]


def _ada(cc, w, b, layer):
    _, d, n = w.shape
    tn = 1024
    return pl.pallas_call(
        _ada_body,
        name="ada_mod",
        grid=(n // tn,),
        in_specs=[pl.BlockSpec((8, d), lambda j: (0, 0)), pl.BlockSpec((None, d, tn), lambda j: (layer, 0, j)),
                  pl.BlockSpec((1, tn), lambda j: (0, j))],
        out_specs=pl.BlockSpec((8, tn), lambda j: (0, j)),
        out_shape=jax.ShapeDtypeStruct((8, n), F32),
        compiler_params=_cparams("parallel"),
    )(cc, w, b)


def _norm_mod(x, g, shift, scale):
    y = x * lax.rsqrt(jnp.mean(x * x, axis=-1, keepdims=True) + EPS) * g
    return y * (1.0 + scale) + shift


def _normmod_body(xl_ref, xc_ref, g_ref, mod_ref, h_ref, *, n_lat_tiles):
    i = pl.program_id(0)

    @pl.when(i < n_lat_tiles)
    def _():
        h_ref[...] = _norm_mod(xl_ref[...], g_ref[...], mod_ref[0:1, :], mod_ref[1:2, :]).astype(BF16)

    @pl.when(i >= n_lat_tiles)
    def _():
        h_ref[...] = _norm_mod(xc_ref[...], g_ref[...], mod_ref[0:1, :], mod_ref[1:2, :]).astype(BF16)


def _normmod(xl, xc, g, mods, seq, bsz):
    nl, nc = xl.shape[0], xc.shape[0]
    d = xl.shape[1]
    tm = _tile(math.gcd(nl, nc), 512)
    n_lat = nl // tm
    return pl.pallas_call(
        functools.partial(_normmod_body, n_lat_tiles=n_lat),
        name="mixer_normmod",
        grid=((nl + nc) // tm,),
        in_specs=[
            pl.BlockSpec((tm, d), lambda i: (jnp.minimum(i, n_lat - 1), 0)),
            pl.BlockSpec((tm, d), lambda i: (jnp.maximum(i - n_lat, 0), 0)),
            pl.BlockSpec((1, d), lambda i: (0, 0)),
            pl.BlockSpec((None, 6, d), lambda i: (_mod_index(i, tm, n_lat, seq, bsz), 0, 0)),
        ],
        out_specs=pl.BlockSpec((tm, d), lambda i: (i, 0)),
        out_shape=jax.ShapeDtypeStruct((nl + nc, d), BF16),
        compiler_params=_cparams("parallel"),
    )(xl, xc, g, mods)


def _small_t_body(p_ref, o_ref):
    o_ref[...] = p_ref[...].T[0:N_SMALL, :]


def _small_t(p):
    rows = p.shape[0]
    tm = _tile(rows, 512)
    return pl.pallas_call(
        _small_t_body,
        name="small_transpose",
        grid=(rows // tm,),
        in_specs=[pl.BlockSpec((tm, LANES), lambda i: (i, P_SMALL // LANES))],
        out_specs=pl.BlockSpec((N_SMALL, tm), lambda i: (0, i)),
        out_shape=jax.ShapeDtypeStruct((N_SMALL, rows), F32),
        compiler_params=_cparams("parallel"),
    )(p)


def _conv_body(u_ref, w_ref, b_ref, s_ref, o_ref, *, act, n_lat_tiles, tile, ctx_len):
    i = pl.program_id(0)
    u = u_ref[...]
    t = lax.broadcasted_iota(I32, u.shape, 0)
    pos = jnp.where(i < n_lat_tiles, t & (GRID_W - 1), lax.rem(t, ctx_len))
    last = jnp.where(i < n_lat_tiles, GRID_W - 1, ctx_len - 1)
    prev = jnp.where(pos == 0, 0.0, pltpu.roll(u, 1, 0))
    nxt = jnp.where(pos == last, 0.0, pltpu.roll(u, tile - 1, 0))
    y = w_ref[0:1, :] * prev + w_ref[1:2, :] * u + w_ref[2:3, :] * nxt + b_ref[...]
    if act:
        y = _silu(y)
    o_ref[...] = y * s_ref[...]


def _short_conv(p, col0, width, w, b, post, act, n_lat_rows, ctx_len, name):
    rows = p.shape[0]
    tile = _tile(math.gcd(n_lat_rows, rows - n_lat_rows), 1024)
    assert tile % ctx_len == 0 and tile % GRID_W == 0
    cb = 512
    return pl.pallas_call(
        functools.partial(_conv_body, act=act, n_lat_tiles=n_lat_rows // tile, tile=tile, ctx_len=ctx_len),
        name=name,
        grid=(rows // tile, width // cb),
        in_specs=[
            pl.BlockSpec((tile, cb), lambda i, j: (i, col0 // cb + j)),
            pl.BlockSpec((3, cb), lambda i, j: (0, j)),
            pl.BlockSpec((1, cb), lambda i, j: (0, j)),
            pl.BlockSpec((1, cb), lambda i, j: (0, j)),
        ],
        out_specs=pl.BlockSpec((tile, cb), lambda i, j: (i, j)),
        out_shape=jax.ShapeDtypeStruct((rows, width), F32),
        compiler_params=_cparams("parallel", "parallel"),
    )(p, w, b, post)


def _hy_filter_body(feat_ref, w1_ref, b1_ref, fr1_ref, w2_ref, b2_ref, fr2_ref, w3a_ref, w3b_ref,
                    da_ref, db_ref, gp_ref, gm_ref, *, seq):
    hdn = jnp.sin(fr1_ref[...] * (_dot3(feat_ref[...], w1_ref[...]) + b1_ref[...]))
    hdn = jnp.sin(fr2_ref[...] * (_dot3(hdn, w2_ref[...]) + b2_ref[...]))
    t = lax.broadcasted_iota(I32, (seq, 1), 0)
    tn = t.astype(F32) / float(seq - 1)
    hf = _dot3(hdn, w3a_ref[...]) * jnp.exp(-tn * jnp.abs(da_ref[...]))
    hb = _dot3(hdn, w3b_ref[...]) * jnp.exp(-tn * jnp.abs(db_ref[...]))
    norm = jnp.sum(jnp.abs(hf) + jnp.abs(hb), axis=0, keepdims=True)
    hf = hf / norm
    hb = jnp.where(t == 0, 0.0, hb / norm)
    gp_ref[...] = hf + hb
    gm_ref[...] = hf - hb


def _hy_filters(seq, w1, b1, fr1, w2, b2, fr2, w3, decay, name):
    t = jnp.arange(seq, dtype=F32)
    freqs = jnp.linspace(1e-4, HY_BANDS - 1, HY_BANDS, dtype=F32)
    ang = (2.0 * math.pi / seq) * t[:, None] * freqs[None, :]
    feats = jnp.concatenate([(t / (seq - 1))[:, None], jnp.cos(ang), -jnp.sin(ang)], axis=-1)
    hp = LANES
    feats = jnp.pad(feats, ((0, 0), (0, hp - HY_EMB)))
    w1p = jnp.pad(w1, ((0, hp - HY_EMB), (0, hp - HY_FFN)))
    w2p = jnp.pad(w2, ((0, hp - HY_FFN), (0, hp - HY_FFN)))
    w3p = jnp.pad(w3, ((0, hp - HY_FFN), (0, 0))).reshape(hp, 4, W).transpose(1, 0, 2)
    row = lambda v: jnp.pad(v, (0, hp - HY_FFN)).reshape(1, hp)
    dec = decay.reshape(4, 1, W)
    cb = 256
    full = lambda shape: pl.BlockSpec(shape, lambda o, j: (0,) * len(shape))
    gp, gm = pl.pallas_call(
        functools.partial(_hy_filter_body, seq=seq),
        name=name,
        grid=(2, W // cb),
        in_specs=[
            full((seq, hp)), full((hp, hp)), full((1, hp)), full((1, hp)),
            full((hp, hp)), full((1, hp)), full((1, hp)),
            pl.BlockSpec((None, hp, cb), lambda o, j: (2 * o, 0, j)),
            pl.BlockSpec((None, hp, cb), lambda o, j: (2 * o + 1, 0, j)),
            pl.BlockSpec((None, 1, cb), lambda o, j: (2 * o, 0, j)),
            pl.BlockSpec((None, 1, cb), lambda o, j: (2 * o + 1, 0, j)),
        ],
        out_specs=[pl.BlockSpec((seq, cb), lambda o, j: (0, o * (W // cb) + j))] * 2,
        out_shape=[jax.ShapeDtypeStruct((seq, 2 * W), F32)] * 2,
        compiler_params=_cparams("parallel", "parallel"),
    )(feats, w1p, row(b1), row(fr1), w2p, row(b2), row(fr2), w3p, w3p, dec, dec)
    return gp, gm


DFT_RADIX = 64


def _dft_body(ca_ref, sa_ref, cb_ref, sb_ref, ea_ref, eb_ref, c_ref, s_ref):
    ea, eb = ea_ref[...], eb_ref[...]
    ca, sa = _sel_dot(ca_ref[...], ea), _sel_dot(sa_ref[...], ea)
    cb, sb = _sel_dot(cb_ref[...], eb), _sel_dot(sb_ref[...], eb)
    c_ref[...] = (ca * cb - sa * sb).astype(BF16)
    s_ref[...] = (sa * cb + ca * sb).astype(BF16)


def _dft_tables(seq):
    rdx = DFT_RADIX
    r = jnp.arange(seq, dtype=I32)[:, None]
    a = jnp.arange(rdx, dtype=I32)[None, :]
    col = jnp.arange(seq, dtype=I32)[None, :]
    ea = jnp.where(col // rdx == a.T, 1.0, 0.0).astype(BF16)
    eb = jnp.where(col % rdx == a.T, 1.0, 0.0).astype(BF16)

    def small(m):
        ang = (m % (4 * seq)).astype(F32) * (math.pi / (2 * seq))
        return jnp.cos(ang), jnp.sin(ang)

    def build(ma, mb, name):
        (ca, sa), (cb, sb) = small(ma), small(mb)
        tr = _tile(seq, 256)
        sm = pl.BlockSpec((tr, rdx), lambda i: (i, 0))
        ex = pl.BlockSpec((rdx, seq), lambda i: (0, 0))
        return pl.pallas_call(
            _dft_body,
            name=name,
            grid=(seq // tr,),
            in_specs=[sm, sm, sm, sm, ex, ex],
            out_specs=[pl.BlockSpec((tr, seq), lambda i: (i, 0))] * 2,
            out_shape=[jax.ShapeDtypeStruct((seq, seq), BF16)] * 2,
            compiler_params=_cparams("parallel"),
        )(ca, sa, cb, sb, ea, eb)

    c, s = build((2 * r + 1) * (rdx * a), (2 * r + 1) * a, "dft_table")
    ct, st = build(r * (2 * rdx * a), r * (2 * a + 1), "dft_table_t")
    return c, s, ct, st


def _hy_fwd_body(c_ref, s_ref, z_ref, gc_ref, gs_ref, p1_ref, p2_ref):
    z = z_ref[...].astype(BF16)
    zc = _dot(c_ref[...], z)
    zs = _dot(s_ref[...], z)
    gc, gs = gc_ref[...], gs_ref[...]
    p1_ref[...] = (zc * gc - zs * gs).astype(BF16)
    p2_ref[...] = (zc * gs + zs * gc).astype(BF16)


def _hy_fwd(ctab, stab, z, zcol, zrow0, gc, gs, gcol, seq, nseq, name):
    tk = _tile(seq, 512)
    nk = seq // tk
    zb0 = zrow0 // seq
    return pl.pallas_call(
        _hy_fwd_body,
        name=name,
        grid=(nseq, nk),
        in_specs=[
            pl.BlockSpec((tk, seq), lambda b, k: (k, 0)),
            pl.BlockSpec((tk, seq), lambda b, k: (k, 0)),
            pl.BlockSpec((seq, W), lambda b, k: (zb0 + b, zcol)),
            pl.BlockSpec((tk, W), lambda b, k: (k, gcol)),
            pl.BlockSpec((tk, W), lambda b, k: (k, gcol)),
        ],
        out_specs=[pl.BlockSpec((tk, W), lambda b, k: (b * nk + k, 0))] * 2,
        out_shape=[jax.ShapeDtypeStruct((nseq * seq, W), BF16)] * 2,
        compiler_params=_cparams("parallel", "arbitrary"),
    )(ctab, stab, z, gc, gs)


def _hy_inv_body(ct_ref, st_ref, p1_ref, p2_ref, zin_ref, mul_ref, bias_ref, o_ref, *, seq):
    y = (_dot(ct_ref[...], p1_ref[...]) + _dot(st_ref[...], p2_ref[...])) * (1.0 / seq)
    o_ref[...] = mul_ref[...] * (y + bias_ref[...] * zin_ref[...])


def _hy_inv(cttab, sttab, p1, p2, zin, zin_col, zin_row0, mul, mul_col, mul_row0, bias, seq, nseq, name):
    tt = _tile(seq, 512)
    nt = seq // tt
    zr, mr = zin_row0 // tt, mul_row0 // tt
    return pl.pallas_call(
        functools.partial(_hy_inv_body, seq=seq),
        name=name,
        grid=(nseq, nt),
        in_specs=[
            pl.BlockSpec((tt, seq), lambda b, t: (t, 0)),
            pl.BlockSpec((tt, seq), lambda b, t: (t, 0)),
            pl.BlockSpec((seq, W), lambda b, t: (b, 0)),
            pl.BlockSpec((seq, W), lambda b, t: (b, 0)),
            pl.BlockSpec((tt, W), lambda b, t: (zr + b * nt + t, zin_col)),
            pl.BlockSpec((tt, W), lambda b, t: (mr + b * nt + t, mul_col)),
            pl.BlockSpec((1, W), lambda b, t: (0, 0)),
        ],
        out_specs=pl.BlockSpec((tt, W), lambda b, t: (b * nt + t, 0)),
        out_shape=jax.ShapeDtypeStruct((nseq * seq, W), F32),
        compiler_params=_cparams("parallel", "arbitrary"),
    )(cttab, sttab, p1, p2, zin, mul, bias)


def _hyena(u, row0, seq, nseq, tabs, filt, tag):
    ctab, stab, cttab, sttab = tabs
    gc, gs, bias = filt
    p1, p2 = _hy_fwd(ctab, stab, u, 0, row0, gc, gs, 0, seq, nseq, "hy_fwd1" + tag)
    z2 = _hy_inv(cttab, sttab, p1, p2, u, 0, row0, u, 1, row0, bias[0:1], seq, nseq, "hy_inv1" + tag)
    p1, p2 = _hy_fwd(ctab, stab, z2, 0, 0, gc, gs, 1, seq, nseq, "hy_fwd2" + tag)
    return _hy_inv(cttab, sttab, p1, p2, z2, 0, 0, u, 2, row0, bias[1:2], seq, nseq, "hy_inv2" + tag)


def _chunk_index(b, c, rev, n_ctx_chunks, n_lat_chunks, bsz):
    in_ctx = c < n_ctx_chunks
    if rev:
        cc = n_ctx_chunks - 1 - c
        lc = n_lat_chunks - 1 - (c - n_ctx_chunks)
    else:
        cc = c
        lc = c - n_ctx_chunks
    return jnp.where(in_ctx, bsz * n_lat_chunks + b * n_ctx_chunks + cc, b * n_lat_chunks + lc)


def _tri(n, rev):
    r = lax.broadcasted_iota(I32, (n, n), 0)
    c = lax.broadcasted_iota(I32, (n, n), 1)
    return (c >= r) if rev else (c <= r)


def _mlstm_chunk(q, k, v, ig_col, ig_row, b_col, b_row, ct, nrow, m, mask, last):
    a_col = b_col + m
    dmat = jnp.where(mask, b_col - b_row + ig_row, -jnp.inf)
    mt = jnp.maximum(a_col, jnp.max(dmat, axis=1, keepdims=True))
    inter = jnp.exp(a_col - mt)
    qb, kb = q.astype(BF16), k.astype(BF16)
    s = _dot_nt(qb, kb) * jnp.exp(dmat - mt)
    num = _dot(s.astype(BF16), v.astype(BF16)) + inter * _dot(qb, ct.astype(BF16))
    den = jnp.sum(s, axis=1, keepdims=True) + inter * jnp.sum(q * nrow, axis=1, keepdims=True)
    h = num / jnp.maximum(jnp.abs(den), jnp.exp(-mt))
    m_new = mt[last:last + 1, :]
    tot = b_col[last:last + 1, :]
    ws = jnp.exp(tot - b_col + ig_col - m_new)
    dec = jnp.exp(tot + m - m_new)
    ct_new = dec * ct + _dot(k.T.astype(BF16), (v * ws).astype(BF16))
    n_new = dec * nrow + jnp.sum(k * ws, axis=0, keepdims=True)
    return h, ct_new, n_new, m_new


def _mlstm_body(qkf_ref, vf_ref, gcf_ref, grf_ref, qkb_ref, vb_ref, gcb_ref, grb_ref, gbc_ref, gbr_ref,
                hf_ref, hb_ref, ct_scr, n_scr, m_scr):
    T = ML_CHUNK

    @pl.when(pl.program_id(1) == 0)
    def _():
        ct_scr[...] = jnp.zeros_like(ct_scr)
        n_scr[...] = jnp.zeros_like(n_scr)
        m_scr[...] = jnp.zeros_like(m_scr)

    dirs = ((qkf_ref, vf_ref, gcf_ref, grf_ref, hf_ref), (qkb_ref, vb_ref, gcb_ref, grb_ref, hb_ref))
    for d, (qk_ref, v_ref, gc_ref, gr_ref, h_ref) in enumerate(dirs):
        rev = d == 1
        mask = _tri(T, rev)
        m01 = jnp.where(mask, 1.0, 0.0).astype(BF16)
        m01t = jnp.where(_tri(T, not rev), 1.0, 0.0).astype(BF16)
        gcol = gc_ref[:, 0:16] + gbc_ref[...]
        grow = gr_ref[0:16, :] + gbr_ref[...]
        ls_col, ls_row = _log_sigmoid(gcol), _log_sigmoid(grow)
        last = 0 if rev else T - 1
        nsub = SCAN_BLOCK // T
        carry = [(ct_scr[d * ML_HEADS + h], n_scr[d * ML_HEADS + h:d * ML_HEADS + h + 1, :],
                  m_scr[d * ML_HEADS + h:d * ML_HEADS + h + 1, 0:1]) for h in range(ML_HEADS)]
        for sub in (range(nsub - 1, -1, -1) if rev else range(nsub)):
            rs = slice(sub * T, (sub + 1) * T)
            cum_col = _dot_sel(m01, ls_col[rs])
            cum_row = _sel_dot(ls_row[:, rs], m01t)
            for h in range(ML_HEADS):
                ci, cf = 2 * d * ML_HEADS + h, (2 * d + 1) * ML_HEADS + h
                hs = slice(h * ML_HD, (h + 1) * ML_HD)
                ks = slice(W + h * ML_HD, W + (h + 1) * ML_HD)
                ct, nrow, m = carry[h]
                hh, ct, nrow, m = _mlstm_chunk(
                    qk_ref[rs, hs], qk_ref[rs, ks], v_ref[rs, hs],
                    gcol[rs, ci:ci + 1], grow[ci:ci + 1, rs], cum_col[:, cf:cf + 1], cum_row[cf:cf + 1, :],
                    ct, nrow, m, mask, last)
                h_ref[rs, hs] = hh
                carry[h] = (ct, nrow, m)
        for h in range(ML_HEADS):
            ct, nrow, m = carry[h]
            j = d * ML_HEADS + h
            ct_scr[j] = ct
            n_scr[j:j + 1, :] = nrow
            m_scr[j:j + 1, :] = jnp.broadcast_to(m, (1, LANES))


def _mlstm_scan(qk, p, small_t, gate_b, bsz, seq, ctx_len):
    T = SCAN_BLOCK
    rows = qk.shape[0]
    ncc, nlc = ctx_len // T, seq // T
    cf = functools.partial(_chunk_index, rev=False, n_ctx_chunks=ncc, n_lat_chunks=nlc, bsz=bsz)
    cr = functools.partial(_chunk_index, rev=True, n_ctx_chunks=ncc, n_lat_chunks=nlc, bsz=bsz)

    def specs(ci):
        return [pl.BlockSpec((T, 2 * W), lambda b, c: (ci(b, c), 0)),
                pl.BlockSpec((T, W), lambda b, c: (ci(b, c), P_MLV // W)),
                pl.BlockSpec((T, LANES), lambda b, c: (ci(b, c), P_SMALL // LANES)),
                pl.BlockSpec((N_SMALL, T), lambda b, c: (0, ci(b, c)))]

    nchain = 2 * ML_HEADS
    return pl.pallas_call(
        _mlstm_body,
        name="mlstm_scan",
        grid=(bsz, ncc + nlc),
        in_specs=specs(cf) + specs(cr) + [pl.BlockSpec((1, 16), lambda b, c: (0, 0)),
                                         pl.BlockSpec((16, 1), lambda b, c: (0, 0))],
        out_specs=[pl.BlockSpec((T, W), lambda b, c: (cf(b, c), 0)), pl.BlockSpec((T, W), lambda b, c: (cr(b, c), 0))],
        out_shape=[jax.ShapeDtypeStruct((rows, W), F32)] * 2,
        scratch_shapes=[pltpu.VMEM((nchain, ML_HD, ML_HD), F32), pltpu.VMEM((nchain, LANES), F32),
                        pltpu.VMEM((nchain, LANES), F32)],
        compiler_params=_cparams("parallel", "arbitrary"),
    )(qk, p, p, small_t, qk, p, p, small_t, gate_b.reshape(1, 16), gate_b.reshape(16, 1))


def _mlstm_fin_body(hf_ref, hb_ref, o_ref, g_ref, y_ref):
    h = hf_ref[...] + hb_ref[...]
    for i in range(ML_HEADS):
        hh = h[:, i * ML_HD:(i + 1) * ML_HD]
        hh = hh * lax.rsqrt(jnp.mean(hh * hh, axis=-1, keepdims=True) + EPS)
        sl = slice(i * ML_HD, (i + 1) * ML_HD)
        y_ref[:, sl] = hh * g_ref[:, sl] * _sigmoid(o_ref[:, sl])


def _mlstm_finish(hf, hb, p, norm_g, rows):
    tm = _tile(rows, 512)
    return pl.pallas_call(
        _mlstm_fin_body,
        name="mlstm_finish",
        grid=(rows // tm,),
        in_specs=[pl.BlockSpec((tm, W), lambda i: (i, 0)), pl.BlockSpec((tm, W), lambda i: (i, 0)),
                  pl.BlockSpec((tm, W), lambda i: (i, P_MLO // W)), pl.BlockSpec((1, W), lambda i: (0, 0))],
        out_specs=pl.BlockSpec((tm, W), lambda i: (i, 0)),
        out_shape=jax.ShapeDtypeStruct((rows, W), F32),
        compiler_params=_cparams("parallel"),
    )(hf, hb, p, norm_g)


def _ssd_body(xf_ref, dcf_ref, drf_ref, xb_ref, dcb_ref, drb_ref, dbc_ref, dbr_ref, ac_ref, ar_ref,
              yf_ref, yb_ref, st_scr):
    T = SSD_CHUNK

    @pl.when(pl.program_id(1) == 0)
    def _():
        st_scr[...] = jnp.zeros_like(st_scr)

    for d, (xbc_ref, dcol_ref, drow_ref, y_ref) in enumerate(((xf_ref, dcf_ref, drf_ref, yf_ref),
                                                              (xb_ref, dcb_ref, drb_ref, yb_ref))):
        rev = d == 1
        mask = _tri(T, rev)
        m01 = jnp.where(mask, 1.0, 0.0).astype(BF16)
        m01t = jnp.where(_tri(T, not rev), 1.0, 0.0).astype(BF16)
        dt_col = _softplus(dcol_ref[:, 16:32] + dbc_ref[...])
        dt_row = _softplus(drow_ref[16:32, :] + dbr_ref[...])
        acs_col = _dot_sel(m01, dt_col * ac_ref[...])
        acs_row = _sel_dot(dt_row * ar_ref[...], m01t)
        last = 0 if rev else T - 1
        for g in range(SSD_GROUPS):
            bm = xbc_ref[:, W + g * SSD_STATE:W + (g + 1) * SSD_STATE]
            cm = xbc_ref[:, W + (SSD_GROUPS + g) * SSD_STATE:W + (SSD_GROUPS + g + 1) * SSD_STATE]
            bmb, cmb = bm.astype(BF16), cm.astype(BF16)
            cb = _dot_nt(cmb, bmb)
            bmt = bm.T.astype(BF16)
            for hh in range(SSD_HEADS // SSD_GROUPS):
                h = g * (SSD_HEADS // SSD_GROUPS) + hh
                ci = d * SSD_HEADS + h
                a_col, a_row = acs_col[:, ci:ci + 1], acs_row[ci:ci + 1, :]
                x = xbc_ref[:, h * SSD_HD:(h + 1) * SSD_HD] * dt_col[:, ci:ci + 1]
                lm = jnp.where(mask, jnp.exp(jnp.where(mask, a_col - a_row, 0.0)), 0.0)
                st = st_scr[ci]
                y = _dot((cb * lm).astype(BF16), x.astype(BF16)) + _dot(cmb, st.astype(BF16)) * jnp.exp(a_col)
                y_ref[:, h * SSD_HD:(h + 1) * SSD_HD] = y
                tot = a_col[last:last + 1, :]
                xd = x * jnp.exp(tot - a_col)
                st_scr[ci] = jnp.exp(tot) * st + _dot(bmt, xd.astype(BF16))


def _ssd_scan(xbc, p, small_t, dt_bias, a_log, bsz, seq, ctx_len):
    T = SSD_CHUNK
    rows = xbc.shape[0]
    ncc, nlc = ctx_len // T, seq // T
    cf = functools.partial(_chunk_index, rev=False, n_ctx_chunks=ncc, n_lat_chunks=nlc, bsz=bsz)
    cr = functools.partial(_chunk_index, rev=True, n_ctx_chunks=ncc, n_lat_chunks=nlc, bsz=bsz)
    a = -jnp.exp(a_log.astype(F32))
    small = lambda shape: pl.BlockSpec(shape, lambda b, c: (0, 0))

    def specs(ci):
        return [pl.BlockSpec((T, 2 * W), lambda b, c: (ci(b, c), 0)),
                pl.BlockSpec((T, LANES), lambda b, c: (ci(b, c), P_SMALL // LANES)),
                pl.BlockSpec((N_SMALL, T), lambda b, c: (0, ci(b, c)))]

    return pl.pallas_call(
        _ssd_body,
        name="ssd_scan",
        grid=(bsz, ncc + nlc),
        in_specs=specs(cf) + specs(cr) + [small((1, 16)), small((16, 1)), small((1, 16)), small((16, 1))],
        out_specs=[pl.BlockSpec((T, W), lambda b, c: (cf(b, c), 0)), pl.BlockSpec((T, W), lambda b, c: (cr(b, c), 0))],
        out_shape=[jax.ShapeDtypeStruct((rows, W), F32)] * 2,
        scratch_shapes=[pltpu.VMEM((2 * SSD_HEADS, SSD_STATE, SSD_HD), F32)],
        compiler_params=_cparams("parallel", "arbitrary"),
    )(xbc, p, small_t, xbc, p, small_t, dt_bias.reshape(1, 16), dt_bias.reshape(16, 1), a.reshape(1, 16),
      a.reshape(16, 1))


def _ssd_fin_body(yf_ref, yb_ref, x_ref, z_ref, dsk_ref, g_ref, o_ref):
    y = yf_ref[...] + yb_ref[...] + dsk_ref[...] * x_ref[...]
    y = y * _silu(z_ref[...])
    o_ref[...] = y * lax.rsqrt(jnp.mean(y * y, axis=-1, keepdims=True) + EPS) * g_ref[...]


def _ssd_finish(yf, yb, xbc, p, d_skip, norm_g, rows):
    tm = _tile(rows, 512)
    blk = lambda col: pl.BlockSpec((tm, W), lambda i: (i, col))
    vec = pl.BlockSpec((1, W), lambda i: (0, 0))
    return pl.pallas_call(
        _ssd_fin_body,
        name="ssd_finish",
        grid=(rows // tm,),
        in_specs=[blk(0), blk(0), blk(0), blk(P_SSDZ // W), vec, vec],
        out_specs=blk(0),
        out_shape=jax.ShapeDtypeStruct((rows, W), F32),
        compiler_params=_cparams("parallel"),
    )(yf, yb, xbc, p, jnp.repeat(d_skip, SSD_HD).reshape(1, W), norm_g)


def _s5_disc_body(are_ref, aim_ref, ldt_ref, bre_ref, bim_ref, abre_ref, abim_ref, bbre_ref, bbim_ref):
    lam_re = jnp.minimum(are_ref[...], -1e-4)
    a_im = aim_ref[...]
    dt = jnp.exp(ldt_ref[...])
    mag = jnp.exp(lam_re * dt)
    ab_re, ab_im = mag * jnp.cos(a_im * dt), mag * jnp.sin(a_im * dt)
    den = lam_re * lam_re + a_im * a_im
    nr, ni = ab_re - 1.0, ab_im
    f_re = (nr * lam_re + ni * a_im) / den
    f_im = (ni * lam_re - nr * a_im) / den
    abre_ref[...] = ab_re
    abim_ref[...] = ab_im
    bbre_ref[...] = f_re * bre_ref[...] - f_im * bim_ref[...]
    bbim_ref[...] = f_re * bim_ref[...] + f_im * bre_ref[...]


def _s5_discretise(a_re, a_im, log_dt, b_re, b_im):
    gn = S5_NS
    col = lambda v: v.reshape(gn, 1)
    ldt = jnp.repeat(log_dt, S5_STATE).reshape(gn, 1)
    mat = lambda v: v.reshape(gn, S5_GROUP)
    cs = pl.BlockSpec((gn, 1), lambda: (0, 0))
    ms = pl.BlockSpec((gn, S5_GROUP), lambda: (0, 0))
    return pl.pallas_call(
        _s5_disc_body,
        name="s5_discretise",
        in_specs=[cs, cs, cs, ms, ms],
        out_specs=[cs, cs, ms, ms],
        out_shape=[jax.ShapeDtypeStruct((gn, 1), F32)] * 2 + [jax.ShapeDtypeStruct((gn, S5_GROUP), F32)] * 2,
    )(col(a_re), col(a_im), ldt, mat(b_re), mat(b_im))


def _s5_body(*refs, nchain, bsz):
    u_refs = refs[:nchain]
    pm_ref, pmt_ref, wb_ref, a_ref, wc_ref, y_ref, x_scr, st_scr = refs[nchain:]
    T = S5_T
    sw = S5_NS // S5_SUPER
    cw = W // S5_SUPER

    @pl.when(pl.program_id(0) == 0)
    def _():
        st_scr[...] = jnp.zeros_like(st_scr)

    stack = jnp.concatenate([r[...] for r in u_refs], axis=0).astype(BF16)
    lhs = _dot(pm_ref[...], stack)
    chain = lax.broadcasted_iota(I32, lhs.shape, 0) % nchain
    lhs_f = jnp.where(chain < bsz, lhs, 0.0).astype(BF16)
    lhs_b = jnp.where(chain < bsz, 0.0, lhs).astype(BF16)
    ys = []
    for g in range(S5_SUPER):
        cs = slice(g * cw, (g + 1) * cw)
        re = slice(g * sw, (g + 1) * sw)
        im = slice(S5_NS + g * sw, S5_NS + (g + 1) * sw)
        bu = _dot(jnp.concatenate([lhs_f[:, cs], lhs_b[:, cs]], axis=1), wb_ref[g])
        x_scr[:, re] = bu[:, :sw]
        x_scr[:, im] = bu[:, sw:]
        ar, ai = a_ref[:, re], a_ref[:, im]

        def step(j, carry):
            sr, si = carry
            r0 = pl.multiple_of(j * nchain, nchain)
            nr = ar * sr - ai * si + x_scr[pl.ds(r0, nchain), re]
            ni = ar * si + ai * sr + x_scr[pl.ds(r0, nchain), im]
            x_scr[pl.ds(r0, nchain), re] = nr
            x_scr[pl.ds(r0, nchain), im] = ni
            return nr, ni

        sr, si = lax.fori_loop(0, T, step, (st_scr[:, re], st_scr[:, im]))
        st_scr[:, re] = sr
        st_scr[:, im] = si
        wc = wc_ref[g]
        ys.append(_dot(x_scr[:, re].astype(BF16), wc[:sw]) + _dot(x_scr[:, im].astype(BF16), wc[sw:]))
    y = _dot_sel(pmt_ref[...], jnp.concatenate(ys, axis=1))
    for c in range(nchain):
        y_ref[c // bsz, c % bsz] = y[c * T:(c + 1) * T]


def _s5_scan(p, pm, pmt, wb, atab, wc, bsz, seq, ctx_len):
    T = S5_T
    nchain = 2 * bsz
    ncc, nlc = ctx_len // T, seq // T
    nsteps = ncc + nlc
    rows = nchain * T
    const = lambda shape: pl.BlockSpec(shape, lambda i: (0,) * len(shape))
    u_specs = []
    for c in range(nchain):
        rev = c >= bsz
        u_specs.append(pl.BlockSpec(
            (T, W), functools.partial(lambda i, b, rev: (_chunk_index(b, i, rev, ncc, nlc, bsz), P_S5 // W),
                                      b=c % bsz, rev=rev)))
    return pl.pallas_call(
        functools.partial(_s5_body, nchain=nchain, bsz=bsz),
        name="s5_scan",
        grid=(nsteps,),
        in_specs=u_specs + [const((rows, rows)), const((rows, rows)), const((S5_SUPER, 2 * W // S5_SUPER, 2 * S5_NS // S5_SUPER)),
                            const((nchain, 2 * S5_NS)), const((S5_SUPER, 2 * S5_NS // S5_SUPER, W // S5_SUPER))],
        out_specs=pl.BlockSpec((2, bsz, None, T, W), lambda i: (0, 0, i, 0, 0)),
        out_shape=jax.ShapeDtypeStruct((2, bsz, nsteps, T, W), F32),
        scratch_shapes=[pltpu.VMEM((rows, 2 * S5_NS), F32), pltpu.VMEM((nchain, 2 * S5_NS), F32)],
        compiler_params=_cparams("arbitrary"),
    )(*([p] * nchain), pm, pmt, wb, atab, wc)


def _s5_fin_body(yf_ref, yb_ref, u_ref, d_ref, w_ref, b_ref, o_ref, *, nchunk):
    yf = yf_ref[...].reshape(nchunk * S5_T, W)
    yb = jnp.concatenate([yb_ref[nchunk - 1 - q] for q in range(nchunk)], axis=0)
    y = yf + yb + d_ref[...] * u_ref[...]
    g = _gelu_tanh(y)
    o_ref[...] = g * _sigmoid(_dot(g.astype(BF16), w_ref[...].astype(BF16)) + b_ref[...])


def _s5_finish(ys, p, d_skip, glu_w, glu_b, rows, bsz, seq, ctx_len):
    tile = ctx_len
    g = tile // S5_T
    nlt = seq // tile
    n_lat = bsz * nlt
    samp = lambda i: jnp.where(i < n_lat, i // nlt, i - n_lat)
    fblk = lambda i: jnp.where(i < n_lat, 1 + i % nlt, 0)
    bblk = lambda i: jnp.where(i < n_lat, nlt - i % nlt, 0)
    vec = pl.BlockSpec((1, W), lambda i: (0, 0))
    return pl.pallas_call(
        functools.partial(_s5_fin_body, nchunk=g),
        name="s5_finish",
        grid=(rows // tile,),
        in_specs=[pl.BlockSpec((None, None, g, S5_T, W), lambda i: (0, samp(i), fblk(i), 0, 0)),
                  pl.BlockSpec((None, None, g, S5_T, W), lambda i: (1, samp(i), bblk(i), 0, 0)),
                  pl.BlockSpec((tile, W), lambda i: (i, P_S5 // W)), vec,
                  pl.BlockSpec((W, W), lambda i: (0, 0)), vec],
        out_specs=pl.BlockSpec((tile, W), lambda i: (i, 0)),
        out_shape=jax.ShapeDtypeStruct((rows, W), F32),
        compiler_params=_cparams("parallel"),
    )(ys, ys, p, d_skip.reshape(1, W), glu_w, glu_b.reshape(1, W))


def _block_diag(m):
    g, a, b = m.shape
    eye = jnp.eye(g, dtype=m.dtype)
    return (eye[:, None, :, None] * m[:, :, None, :]).reshape(g * a, g * b)


def _s5_branch(p, rows, bsz, seq, ctx_len, a_re, a_im, log_dt, b_re, b_im, c_re, c_im, d_skip, glu_w, glu_b):
    disc = [_s5_discretise(a_re[d], a_im[d], log_dt[d], b_re, b_im) for d in range(2)]
    gs = S5_GROUPS // S5_SUPER
    def bmat(v):
        v = v.reshape(S5_SUPER, gs, S5_STATE, S5_GROUP).transpose(0, 1, 3, 2)
        return jnp.stack([_block_diag(v[s]) for s in range(S5_SUPER)])
    wb = jnp.concatenate([jnp.concatenate([bmat(disc[d][2]), bmat(disc[d][3])], axis=2) for d in range(2)],
                         axis=1).astype(BF16)
    arow = lambda d: jnp.concatenate([disc[d][0].reshape(1, S5_NS), disc[d][1].reshape(1, S5_NS)], axis=1)
    atab = jnp.concatenate([jnp.broadcast_to(arow(0), (bsz, 2 * S5_NS)),
                            jnp.broadcast_to(arow(1), (bsz, 2 * S5_NS))], axis=0)
    def cmat(v):
        v = v.reshape(S5_SUPER, gs, S5_GROUP, S5_STATE).transpose(0, 1, 3, 2)
        return jnp.stack([_block_diag(v[s]) for s in range(S5_SUPER)])
    wc = jnp.concatenate([cmat(c_re), -cmat(c_im)], axis=1).astype(BF16)
    T, nchain = S5_T, 2 * bsz
    r = jnp.arange(nchain * T, dtype=I32)
    j, c = r // nchain, r % nchain
    src = c * T + jnp.where(c < bsz, j, T - 1 - j)
    pm = (src[:, None] == r[None, :])
    ys = _s5_scan(p, pm.astype(BF16), pm.T.astype(BF16), wb, atab, wc, bsz, seq, ctx_len)
    return _s5_finish(ys, p, d_skip, glu_w, glu_b, rows, bsz, seq, ctx_len)


def _merge_body(y0l, y0c, y1, y2, y3, g0, g1, g2, g3, w0, w1, w2, w3, o_ref, *, n_lat_tiles):
    hy = jnp.where(pl.program_id(0) < n_lat_tiles, y0l[...], y0c[...])
    acc = _sigmoid(g0[...]) * _dot(hy.astype(BF16), w0[...])
    for y, g, w in ((y1, g1, w1), (y2, g2, w2), (y3, g3, w3)):
        acc = acc + _sigmoid(g[...]) * _dot(y[...].astype(BF16), w[...])
    o_ref[...] = acc.astype(BF16)


def _merge(hy_lat, hy_ctx, ys, p, wb, rows):
    nl = hy_lat.shape[0]
    tm, tn = _tile(math.gcd(nl, hy_ctx.shape[0]), 512), 512
    n_lat = nl // tm
    ysp = pl.BlockSpec((tm, W), lambda i, j: (i, 0))
    gsp = lambda k: pl.BlockSpec((tm, tn), lambda i, j: (i, (P_GATE + k * D_MODEL) // tn + j))
    wsp = lambda k: pl.BlockSpec((None, W, tn), lambda i, j: (k, 0, j))
    return pl.pallas_call(
        functools.partial(_merge_body, n_lat_tiles=n_lat),
        name="merge",
        grid=(rows // tm, D_MODEL // tn),
        in_specs=[pl.BlockSpec((tm, W), lambda i, j: (jnp.minimum(i, n_lat - 1), 0)),
                  pl.BlockSpec((tm, W), lambda i, j: (jnp.maximum(i - n_lat, 0), 0))]
                 + [ysp] * 3 + [gsp(k) for k in range(4)] + [wsp(k) for k in range(4)],
        out_specs=pl.BlockSpec((tm, tn), lambda i, j: (i, j)),
        out_shape=jax.ShapeDtypeStruct((rows, D_MODEL), BF16),
        compiler_params=_cparams("parallel", "parallel"),
    )(hy_lat, hy_ctx, *ys, p, p, p, p, wb, wb, wb, wb)


def _outproj_body(a_ref, w_ref, xl_ref, xc_ref, mod_ref, o_ref, *, n_lat_tiles):
    y = mod_ref[2:3, :] * _dot(a_ref[...], w_ref[...])
    i = pl.program_id(0)

    @pl.when(i < n_lat_tiles)
    def _():
        o_ref[...] = xl_ref[...] + y

    @pl.when(i >= n_lat_tiles)
    def _():
        o_ref[...] = xc_ref[...] + y


def _outproj(acc, w, xl, xc, mods, seq, bsz):
    rows = acc.shape[0]
    nl, nc = xl.shape[0], xc.shape[0]
    tm = _tile(math.gcd(nl, nc), 512)
    tn = D_MODEL
    n_lat = nl // tm
    return pl.pallas_call(
        functools.partial(_outproj_body, n_lat_tiles=n_lat),
        name="outproj",
        grid=(rows // tm, D_MODEL // tn),
        in_specs=[
            pl.BlockSpec((tm, D_MODEL), lambda i, j: (i, 0)),
            pl.BlockSpec((D_MODEL, tn), lambda i, j: (0, j)),
            pl.BlockSpec((tm, tn), lambda i, j: (jnp.minimum(i, n_lat - 1), j)),
            pl.BlockSpec((tm, tn), lambda i, j: (jnp.maximum(i - n_lat, 0), j)),
            pl.BlockSpec((None, 6, tn), lambda i, j: (_mod_index(i, tm, n_lat, seq, bsz), 0, j)),
        ],
        out_specs=pl.BlockSpec((tm, tn), lambda i, j: (i, j)),
        out_shape=jax.ShapeDtypeStruct((rows, D_MODEL), F32),
        compiler_params=_cparams("parallel", "parallel"),
    )(acc, w, xl, xc, mods)


def _router_body(x_ref, g_ref, mod_ref, wr_ref, h_ref, p_ref):
    h = _norm_mod(x_ref[...], g_ref[...], mod_ref[3:4, :], mod_ref[4:5, :])
    h_ref[...] = h.astype(BF16)
    logits = _dot3(h, wr_ref[...])
    lane = lax.broadcasted_iota(I32, logits.shape, 1)
    logits = jnp.where(lane < N_EXPERTS, logits, -jnp.inf)
    e = jnp.exp(logits - jnp.max(logits, axis=-1, keepdims=True))
    p_ref[...] = e / jnp.sum(e, axis=-1, keepdims=True)


def _router(xm, g, mods, w_router, n_lat_rows, seq, bsz):
    rows = xm.shape[0]
    tm = _tile(math.gcd(n_lat_rows, seq), 512)
    n_lat = n_lat_rows // tm
    wr = jnp.pad(w_router, ((0, 0), (0, LANES - N_EXPERTS)))
    return pl.pallas_call(
        _router_body,
        name="moe_router",
        grid=(rows // tm,),
        in_specs=[
            pl.BlockSpec((tm, D_MODEL), lambda i: (i, 0)),
            pl.BlockSpec((1, D_MODEL), lambda i: (0, 0)),
            pl.BlockSpec((None, 6, D_MODEL), lambda i: (_mod_index(i, tm, n_lat, seq, bsz), 0, 0)),
            pl.BlockSpec((D_MODEL, LANES), lambda i: (0, 0)),
        ],
        out_specs=[pl.BlockSpec((tm, D_MODEL), lambda i: (i, 0)), pl.BlockSpec((tm, LANES), lambda i: (i, 0))],
        out_shape=[jax.ShapeDtypeStruct((rows, D_MODEL), BF16), jax.ShapeDtypeStruct((rows, LANES), F32)],
        compiler_params=_cparams("parallel"),
    )(xm, g, mods, wr)


def _route_geometry(n):
    cap = EC_CAPACITY * n // N_EXPERTS
    blk = min(n, ROUTE_BLOCK)
    nblk = n // blk
    nb1 = -(-(nblk + 1) // 8) * 8
    st = min(cap, LANES)
    return cap, blk, nblk, nb1, st


def _topk_body(p_ref, o_ref, cnt_ref, *, n, cap, blk):
    bits = pltpu.bitcast(p_ref[...], I32)
    capf = float(cap)

    def count(mask):
        return jnp.sum(jnp.where(mask, 1.0, 0.0), axis=0, keepdims=True)

    def vstep(i, thr):
        cand = thr | lax.shift_left(jnp.int32(1), 30 - i)
        return jnp.where(count(bits >= cand) >= capf, cand, thr)

    thr = lax.fori_loop(0, 31, vstep, jnp.zeros((1, LANES), I32))
    gt = bits > thr
    eq = bits == thr
    need = capf - count(gt)
    t = lax.broadcasted_iota(I32, bits.shape, 0)
    nbits = max(1, (n - 1).bit_length())

    def istep(i, j):
        cand = j + lax.shift_left(jnp.int32(1), nbits - 1 - i)
        return jnp.where(count(eq & (t < cand)) < need, cand, j)

    jmax = lax.fori_loop(0, nbits, istep, jnp.zeros((1, LANES), I32))
    sel = gt | (eq & (t <= jmax))
    self32 = jnp.where(sel, 1.0, 0.0)
    r = lax.broadcasted_iota(I32, (blk, blk), 0)
    c = lax.broadcasted_iota(I32, (blk, blk), 1)
    lower = jnp.where(c < r, 1.0, 0.0).astype(BF16)
    carry = jnp.zeros((1, LANES), F32)
    cnt_ref[...] = jnp.zeros_like(cnt_ref)
    for i in range(n // blk):
        sb = self32[i * blk:(i + 1) * blk]
        rank = _dot(lower, sb.astype(BF16)) + carry
        o_ref[i * blk:(i + 1) * blk, :] = jnp.where(sb > 0.0, rank, -1.0).astype(I32)
        carry = carry + jnp.sum(sb, axis=0, keepdims=True)
        cnt_ref[i + 1:i + 2, :] = carry.astype(I32)


def _topk(probs, row0, n, nsets, name):
    cap, blk, nblk, nb1, _ = _route_geometry(n)
    return pl.pallas_call(
        functools.partial(_topk_body, n=n, cap=cap, blk=blk),
        name=name,
        grid=(nsets,),
        in_specs=[pl.BlockSpec((n, LANES), lambda s: (row0 // n + s, 0))],
        out_specs=[pl.BlockSpec((n, LANES), lambda s: (s, 0)), pl.BlockSpec((None, nb1, LANES), lambda s: (s, 0, 0))],
        out_shape=[jax.ShapeDtypeStruct((nsets * n, LANES), I32), jax.ShapeDtypeStruct((nsets, nb1, LANES), I32)],
        compiler_params=_cparams("parallel"),
    )(probs)


def _gather_body(cnt_ref, slot_ref, p_ref, h_ref, xs_ref, gate_ref, acc, gacc, *, cap, blk, nblk, nb1, st):
    s = pl.program_id(0)
    e = pl.program_id(2)
    bounds = [cnt_ref[(s * nb1 + j) * N_EXPERTS + e] for j in range(nblk + 1)]
    lane = lax.broadcasted_iota(I32, (st, LANES), 1)
    for q in range(cap // st):
        lo, hi = q * st, (q + 1) * st
        b_lo = sum((bounds[j + 1] <= lo).astype(I32) for j in range(nblk))
        b_hi = sum((bounds[j] < hi).astype(I32) for j in range(nblk))
        rr = lax.broadcasted_iota(I32, (st, blk), 0) + lo
        acc[...] = jnp.zeros_like(acc)
        gacc[...] = jnp.zeros_like(gacc)

        def body(j, _):
            r0 = pl.multiple_of(j * blk, blk)
            onehot = jnp.where(slot_ref[e, pl.ds(j, 1), :] == rr, 1.0, 0.0).astype(BF16)
            acc[...] += _dot(onehot, h_ref[pl.ds(r0, blk), :])
            gacc[...] += _dot_sel(onehot, p_ref[pl.ds(r0, blk), :])
            return 0

        lax.fori_loop(b_lo, b_hi, body, 0)
        xs_ref[lo:hi, :] = acc[...].astype(BF16)
        gate_ref[lo:hi, :] = jnp.sum(jnp.where(lane == e, gacc[...], 0.0), axis=1, keepdims=True)


def _gather(slot_col, counts, probs, h2, row0, n, nsets, name):
    cap, blk, nblk, nb1, st = _route_geometry(n)
    dh = D_MODEL // 2
    slot_row = slot_col.reshape(nsets, nblk, blk, LANES)[..., :N_EXPERTS].transpose(0, 3, 1, 2)
    cnt = counts[:, :, :N_EXPERTS].reshape(-1)
    grid_spec = pltpu.PrefetchScalarGridSpec(
        num_scalar_prefetch=1,
        grid=(nsets, 2, N_EXPERTS),
        in_specs=[
            pl.BlockSpec((None, N_EXPERTS, nblk, blk), lambda s, k, e, c: (s, 0, 0, 0)),
            pl.BlockSpec((n, LANES), lambda s, k, e, c: (row0 // n + s, 0)),
            pl.BlockSpec((n, dh), lambda s, k, e, c: (row0 // n + s, k)),
        ],
        out_specs=[pl.BlockSpec((None, cap, dh), lambda s, k, e, c: (e, s, k)),
                   pl.BlockSpec((None, None, cap, 1), lambda s, k, e, c: (k, e, s, 0))],
        scratch_shapes=[pltpu.VMEM((st, dh), F32), pltpu.VMEM((st, LANES), F32)],
    )
    xs, gate = pl.pallas_call(
        functools.partial(_gather_body, cap=cap, blk=blk, nblk=nblk, nb1=nb1, st=st),
        name=name,
        grid_spec=grid_spec,
        out_shape=[jax.ShapeDtypeStruct((N_EXPERTS, nsets * cap, D_MODEL), BF16),
                   jax.ShapeDtypeStruct((2, N_EXPERTS, nsets * cap, 1), F32)],
        compiler_params=_cparams("parallel", "arbitrary", "arbitrary"),
    )(cnt, slot_row, probs, h2)
    return xs, gate[0]


def _ffn_body(*refs, with_ctx, ml):
    if with_ctx:
        xl_ref, gl_ref, xc_ref, gc_ref, wg_ref, wu_ref, wd_ref, yl_ref, yc_ref, acc = refs
        xs = jnp.concatenate([xl_ref[...], xc_ref[...]], axis=0)
    else:
        xl_ref, gl_ref, wg_ref, wu_ref, wd_ref, yl_ref, acc = refs
        xs = xl_ref[...]
    f = pl.program_id(2)

    @pl.when(f == 0)
    def _():
        acc[...] = jnp.zeros_like(acc)

    hid = _silu(_dot(xs, wg_ref[...].astype(BF16))) * _dot(xs, wu_ref[...].astype(BF16))
    acc[...] += _dot(hid.astype(BF16), wd_ref[...].astype(BF16))

    @pl.when(f == pl.num_programs(2) - 1)
    def _():
        yl_ref[...] = (acc[0:ml, :] * gl_ref[...]).astype(BF16)
        if with_ctx:
            yc_ref[...] = (acc[ml:, :] * gc_ref[...]).astype(BF16)


def _ffn(xs_l, g_l, xs_c, g_c, w_gate, w_up, w_down, layer):
    with_ctx = xs_c is not None
    nsplit = 2
    ml = xs_l.shape[1] // nsplit
    mc = xs_c.shape[1] // nsplit if with_ctx else 0
    fc = 256
    row = lambda m, last: pl.BlockSpec((None, m, last), lambda e, s, f: (e, s, 0))
    in_specs = [row(ml, D_MODEL), row(ml, 1)]
    args = [xs_l, g_l]
    out_specs = [row(ml, D_MODEL)]
    out_shape = [jax.ShapeDtypeStruct(xs_l.shape, BF16)]
    if with_ctx:
        in_specs += [row(mc, D_MODEL), row(mc, 1)]
        args += [xs_c, g_c]
        out_specs.append(row(mc, D_MODEL))
        out_shape.append(jax.ShapeDtypeStruct(xs_c.shape, BF16))
    in_specs += [pl.BlockSpec((None, None, D_MODEL, fc), lambda e, s, f: (layer, e, 0, f)),
                 pl.BlockSpec((None, None, D_MODEL, fc), lambda e, s, f: (layer, e, 0, f)),
                 pl.BlockSpec((None, None, fc, D_MODEL), lambda e, s, f: (layer, e, f, 0))]
    args += [w_gate, w_up, w_down]
    out = pl.pallas_call(
        functools.partial(_ffn_body, with_ctx=with_ctx, ml=ml),
        name="moe_ffn",
        grid=(N_EXPERTS, nsplit, D_EXPERT // fc),
        in_specs=in_specs,
        out_specs=out_specs,
        out_shape=out_shape,
        scratch_shapes=[pltpu.VMEM((ml + mc, D_MODEL), F32)],
        compiler_params=_cparams("parallel", "parallel", "arbitrary"),
    )(*args)
    return (out[0], out[1]) if with_ctx else (out[0], None)


def _scatter_body(cnt_ref, slot_ref, y_ref, x_ref, mod_ref, o_ref, acc, *, st, nb1, bpt):
    s, t, e = pl.program_id(0), pl.program_id(1), pl.program_id(2)

    @pl.when(e == 0)
    def _():
        acc[...] = jnp.zeros_like(acc)

    lo = cnt_ref[(s * nb1 + t * bpt) * N_EXPERTS + e]
    hi = cnt_ref[(s * nb1 + (t + 1) * bpt) * N_EXPERTS + e]
    shift = st.bit_length() - 1
    k_lo = lax.shift_right_logical(lo, shift)
    k_hi = lax.shift_right_logical(hi + (st - 1), shift)
    slot = slot_ref[...]
    lane = lax.broadcasted_iota(I32, slot.shape, 1)
    col = jnp.sum(jnp.where(lane == e, slot, 0).astype(F32), axis=1, keepdims=True)
    r = lax.broadcasted_iota(I32, (slot.shape[0], st), 1).astype(F32)

    def body(kb, _):
        k0 = pl.multiple_of(kb * st, st)
        onehot = jnp.where(col == r + k0.astype(F32), 1.0, 0.0).astype(BF16)
        acc[...] += _dot(onehot, y_ref[pl.ds(k0, st), :])
        return 0

    lax.fori_loop(k_lo, k_hi, body, 0)

    @pl.when(e == pl.num_programs(2) - 1)
    def _():
        o_ref[...] = x_ref[...] + mod_ref[5:6, :] * acc[...]


def _scatter(slot_col, counts, y, xm, mods, row0, n, nsets, mod_of_set, name):
    cap, blk, nblk, nb1, st = _route_geometry(n)
    tt = _tile(n, 1024)
    nt = n // tt
    cnt = counts[:, :, :N_EXPERTS].reshape(-1)
    grid_spec = pltpu.PrefetchScalarGridSpec(
        num_scalar_prefetch=1,
        grid=(nsets, nt, N_EXPERTS),
        in_specs=[
            pl.BlockSpec((tt, LANES), lambda s, t, e, c: (s * nt + t, 0)),
            pl.BlockSpec((None, cap, D_MODEL), lambda s, t, e, c: (e, s, 0)),
            pl.BlockSpec((tt, D_MODEL), lambda s, t, e, c: (row0 // tt + s * nt + t, 0)),
            pl.BlockSpec((None, 6, D_MODEL), lambda s, t, e, c: (mod_of_set(s), 0, 0)),
        ],
        out_specs=pl.BlockSpec((tt, D_MODEL), lambda s, t, e, c: (s * nt + t, 0)),
        scratch_shapes=[pltpu.VMEM((tt, D_MODEL), F32)],
    )
    return pl.pallas_call(
        functools.partial(_scatter_body, st=st, nb1=nb1, bpt=tt // blk),
        name=name,
        grid_spec=grid_spec,
        out_shape=jax.ShapeDtypeStruct((nsets * n, D_MODEL), F32),
        compiler_params=_cparams("parallel", "parallel", "arbitrary"),
    )(cnt, slot_col, y, xm, mods)


def _moe(xm, g, mods, w_router, w_gate, w_up, w_down, layer, n_lat_rows, bsz, seq, ctx_len, with_ctx):
    h2, probs = _router(xm, g, mods, w_router, n_lat_rows, seq, bsz)

    def route(row0, n, tag):
        slot_col, counts = _topk(probs, row0, n, bsz, "moe_topk" + tag)
        xs, gate = _gather(slot_col, counts, probs, h2, row0, n, bsz, "moe_gather" + tag)
        return slot_col, counts, xs, gate

    slot_l, cnt_l, xs_l, gate_l = route(0, seq, "_lat")
    if with_ctx:
        slot_c, cnt_c, xs_c, gate_c = route(n_lat_rows, ctx_len, "_ctx")
    else:
        xs_c = gate_c = None
    y_l, y_c = _ffn(xs_l, gate_l, xs_c, gate_c, w_gate, w_up, w_down, layer)
    xl = _scatter(slot_l, cnt_l, y_l, xm, mods, 0, seq, bsz, lambda s: s, "moe_scatter_lat")
    xc = None
    if with_ctx:
        xc = _scatter(slot_c, cnt_c, y_c, xm, mods, n_lat_rows, ctx_len, bsz, lambda s: bsz, "moe_scatter_ctx")
    return xl, xc


def _final_norm_body(x_ref, g_ref, o_ref):
    x = x_ref[...]
    o_ref[...] = x * lax.rsqrt(jnp.mean(x * x, axis=-1, keepdims=True) + EPS) * g_ref[...]


def _final_norm(x, g):
    rows, d = x.shape
    tm = _tile(rows, 512)
    return pl.pallas_call(
        _final_norm_body,
        name="final_norm",
        grid=(rows // tm,),
        in_specs=[pl.BlockSpec((tm, d), lambda i: (i, 0)), pl.BlockSpec((1, d), lambda i: (0, 0))],
        out_specs=pl.BlockSpec((tm, d), lambda i: (i, 0)),
        out_shape=jax.ShapeDtypeStruct((rows, d), F32),
        compiler_params=_cparams("parallel"),
    )(x, g)


def _reorder_w_in(w):
    pad = jnp.zeros((w.shape[0], P_WIDTH - P_SMALL - N_SMALL), w.dtype)
    return jnp.concatenate([w[:, 0:3584], w[:, 3600:5648], w[:, 5664:13856], w[:, 3584:3600], w[:, 5648:5664], pad],
                           axis=1).astype(BF16)


def _mixer(xl, xc, mods, lw, tabs_lat, tabs_ctx, bsz, seq, ctx_len, with_ctx_out):
    n_lat_rows = bsz * seq
    rows_all = n_lat_rows + bsz * ctx_len
    rows = rows_all if with_ctx_out else n_lat_rows
    h = _normmod(xl, xc, lw["norm_mix"], mods, seq, bsz)
    p = _matmul(h, lw["w_in"], 1024, 1024, "inproj")
    small_t = _small_t(p)
    ones = lambda n: jnp.ones((1, n), F32)
    u_hy = _short_conv(p, P_HY, 3 * W, lw["hy_conv_w"], lw["hy_conv_b"].reshape(1, -1), ones(3 * W), False,
                       n_lat_rows, ctx_len, "conv_hyena")
    hy_args = (lw["hy_w1"], lw["hy_b1"], lw["hy_fr1"], lw["hy_w2"], lw["hy_b2"], lw["hy_fr2"], lw["hy_w3"],
               lw["hy_decay"])

    def hy_filter(n, tabs, tag):
        gp, gm = _hy_filters(n, *hy_args, "hy_filter" + tag)
        return (_matmul(tabs[0], gp, 512, 512, "hy_spec_c" + tag), _matmul(tabs[1], gm, 512, 512, "hy_spec_s" + tag),
                lw["hy_bias"])

    hy_lat = _hyena(u_hy, 0, seq, bsz, tabs_lat, hy_filter(seq, tabs_lat, "_lat"), "_lat")
    hy_ctx = hy_lat
    if with_ctx_out:
        hy_ctx = _hyena(u_hy, n_lat_rows, ctx_len, bsz, tabs_ctx, hy_filter(ctx_len, tabs_ctx, "_ctx"), "_ctx")
    post = jnp.concatenate([jnp.ones((1, W), F32), jnp.full((1, W), ML_HD ** -0.5, F32)], axis=1)
    qk = _short_conv(p, P_MLQK, 2 * W, lw["ml_conv_w"], lw["ml_conv_b"].reshape(1, -1), post, True, n_lat_rows,
                     ctx_len, "conv_mlstm")
    hf, hb = _mlstm_scan(qk, p, small_t, lw["ml_gate_b"].reshape(16), bsz, seq, ctx_len)
    y_ml = _mlstm_finish(hf, hb, p, lw["ml_norm"].reshape(1, W), rows)
    y_s5 = _s5_branch(p, rows, bsz, seq, ctx_len, lw["s5_a_re"], lw["s5_a_im"], lw["s5_log_dt"], lw["s5_b_re"],
                      lw["s5_b_im"], lw["s5_c_re"], lw["s5_c_im"], lw["s5_d"], lw["s5_glu_w"], lw["s5_glu_b"])
    xbc = _short_conv(p, P_XBC, 2 * W, lw["ssd_conv_w"], lw["ssd_conv_b"].reshape(1, -1), ones(2 * W), True,
                      n_lat_rows, ctx_len, "conv_ssd")
    sf, sb = _ssd_scan(xbc, p, small_t, lw["ssd_dt_bias"].reshape(16), lw["ssd_a_log"].reshape(16), bsz, seq, ctx_len)
    y_ssd = _ssd_finish(sf, sb, xbc, p, lw["ssd_d"], lw["ssd_norm"].reshape(1, W), rows)
    acc = _merge(hy_lat, hy_ctx, (y_ml, y_s5, y_ssd), p, lw["w_branch"], rows)
    return _outproj(acc, lw["w_out"], xl, xc, mods, seq, bsz)


_PER_LAYER = ("ada_w", "ada_b", "norm_mix", "norm_ffn", "w_in", "hy_conv_w", "hy_conv_b", "hy_w1", "hy_b1", "hy_fr1",
              "hy_w2", "hy_b2", "hy_fr2", "hy_w3", "hy_decay", "hy_bias", "ml_conv_w", "ml_conv_b", "ml_gate_b",
              "ml_norm", "s5_a_re", "s5_a_im", "s5_log_dt", "s5_b_re", "s5_b_im", "s5_c_re", "s5_c_im", "s5_d",
              "s5_glu_w", "s5_glu_b", "ssd_conv_w", "ssd_conv_b", "ssd_dt_bias", "ssd_a_log", "ssd_d", "ssd_norm",
              "w_branch", "w_out", "w_router", "w_gate", "w_up", "w_down")


def kernel(x, c, ctx, c_ctx, ada_w, ada_b, norm_mix, norm_ffn, w_in, hy_conv_w, hy_conv_b, hy_w1, hy_b1, hy_fr1, hy_w2, hy_b2, hy_fr2, hy_w3, hy_decay, hy_bias, ml_conv_w, ml_conv_b, ml_gate_b, ml_norm, s5_a_re, s5_a_im, s5_log_dt, s5_b_re, s5_b_im, s5_c_re, s5_c_im, s5_d, s5_glu_w, s5_glu_b, ssd_conv_w, ssd_conv_b, ssd_dt_bias, ssd_a_log, ssd_d, ssd_norm, w_branch, w_out, w_router, w_gate, w_up, w_down, final_norm):
    stacked = dict(zip(_PER_LAYER, (ada_w, ada_b, norm_mix, norm_ffn, w_in, hy_conv_w, hy_conv_b, hy_w1, hy_b1, hy_fr1,
                                    hy_w2, hy_b2, hy_fr2, hy_w3, hy_decay, hy_bias, ml_conv_w, ml_conv_b, ml_gate_b,
                                    ml_norm, s5_a_re, s5_a_im, s5_log_dt, s5_b_re, s5_b_im, s5_c_re, s5_c_im, s5_d,
                                    s5_glu_w, s5_glu_b, ssd_conv_w, ssd_conv_b, ssd_dt_bias, ssd_a_log, ssd_d, ssd_norm,
                                    w_branch, w_out, w_router, w_gate, w_up, w_down)))
    bsz, seq, d = x.shape
    ctx_len = ctx.shape[1]
    depth = ada_w.shape[0]
    assert d == D_MODEL and 2 * bsz <= 8 and seq % ctx_len == 0 and ctx_len % SCAN_BLOCK == 0
    xl = x.reshape(bsz * seq, d)
    xc = ctx.reshape(bsz * ctx_len, d)
    cc = jnp.zeros((8, d), F32).at[:bsz].set(c).at[bsz].set(c_ctx)
    tabs_lat = _dft_tables(seq)
    tabs_ctx = _dft_tables(ctx_len)
    for i in range(depth):
        last = i == depth - 1
        lw = {k: v[i] for k, v in stacked.items() if k not in ("ada_w", "w_gate", "w_up", "w_down")}
        lw["w_in"] = _reorder_w_in(lw["w_in"])
        lw["norm_mix"] = lw["norm_mix"].reshape(1, d)
        lw["norm_ffn"] = lw["norm_ffn"].reshape(1, d)
        lw["w_branch"] = lw["w_branch"].astype(BF16)
        lw["w_out"] = lw["w_out"].astype(BF16)
        mods = _ada(cc, ada_w, lw["ada_b"].reshape(1, -1), i).reshape(8, 6, d)
        xm = _mixer(xl, xc, mods, lw, tabs_lat, tabs_ctx, bsz, seq, ctx_len, not last)
        xl, xc_new = _moe(xm, lw["norm_ffn"], mods, lw["w_router"], w_gate, w_up, w_down, i,
                          bsz * seq, bsz, seq, ctx_len, not last)
        if not last:
            xc = xc_new
    return _final_norm(xl, final_norm.reshape(1, d)).reshape(bsz, seq, d)
```
